```python
import functools
import jax, jax.numpy as jnp
from jax import lax
import numpy as np


D_MODEL = 1024
BATCH = 16
SEQ = 2048
DEPTH = 2

GRID_W = 64
CTX_LEN = 256
N_MIXERS = 2
N_MOD = 6
EPS = 1e-6
GLA_HEADS = 4
GLA_KEY_DIM = D_MODEL // 2
GLA_VAL_DIM = D_MODEL
GLA_HEAD_K = GLA_KEY_DIM // GLA_HEADS
GLA_HEAD_V = GLA_VAL_DIM // GLA_HEADS
GLA_GATE_RANK = 16
GLA_GATE_TAU = 16.0
GLA_CHUNK = 64
GLA_IN_DIM = 2 * GLA_KEY_DIM + 2 * GLA_VAL_DIM + 2 * GLA_GATE_RANK
SC_DIM = D_MODEL
CONV_WIDTH = 3
FFN_HIDDEN = 5 * D_MODEL // 2

kernel_name = 'hybrid_gla_shortconv_convffn_dit'


def rmsnorm(x, gain):
    x32 = x.astype(jnp.float32)
    y = x32 * lax.rsqrt(jnp.mean(x32 * x32, axis=-1, keepdims=True) + EPS)
    return y.astype(x.dtype) * gain


def modulate(x, gain, shift, scale):
    return rmsnorm(x, gain) * (1 + scale) + shift


def dwconv3(u, w, axis):
    n = u.shape[axis]
    pad = [(0, 0)] * u.ndim
    pad[axis] = (1, 1)
    up = jnp.pad(u, pad)
    out = lax.slice_in_dim(up, 0, n, axis=axis) * w[0]
    for tap in range(1, CONV_WIDTH):
        out = out + lax.slice_in_dim(up, tap, tap + n, axis=axis) * w[tap]
    return out


def conv_grid(u, w, rows, axis):
    b, t, ch = u.shape
    return dwconv3(u.reshape(b, rows, GRID_W, ch), w, axis).reshape(b, t, ch)


def conv_seq(u, w):
    return dwconv3(u, w, 1)


def heads(t, dh):
    return t.reshape(t.shape[0], t.shape[1], -1, dh)


def gla_log_decay(a_low, w_a2, b_a):
    z = (a_low @ w_a2 + b_a).astype(jnp.float32)
    return heads(jax.nn.log_sigmoid(z) / GLA_GATE_TAU, GLA_HEAD_K)


def gla_scan(q, k, v, log_a, s0):
    bsz, t, nh, _ = q.shape
    dv = v.shape[-1]
    n = t // GLA_CHUNK

    def to_chunks(a):
        return a.astype(jnp.float32).reshape(bsz, n, GLA_CHUNK, nh, a.shape[-1]).transpose(1, 0, 3, 2, 4)

    xs = tuple(to_chunks(a) for a in (q, k, v, log_a))
    mask = jnp.tril(jnp.ones((GLA_CHUNK, GLA_CHUNK), dtype=bool))

    def step(s, inp):
        qi, ki, vi, gi = inp
        bcum = jnp.cumsum(gi, axis=-2)
        b_last = bcum[..., -1:, :]
        q_s = qi * jnp.exp(bcum)
        k_s = ki * jnp.exp(-bcum)
        k_d = ki * jnp.exp(b_last - bcum)
        att = jnp.where(mask, jnp.einsum('bhik,bhjk->bhij', q_s, k_s), 0.0)
        o = jnp.einsum('bhik,bhkv->bhiv', q_s, s) + jnp.einsum('bhij,bhjv->bhiv', att, vi)
        s_new = jnp.exp(b_last[..., 0, :])[..., None] * s + jnp.einsum('bhjk,bhjv->bhkv', k_d, vi)
        return s_new, o

    s_fin, oc = lax.scan(step, s0.astype(jnp.float32), xs)
    o = oc.transpose(1, 0, 3, 2, 4).reshape(bsz, t, nh, dv)
    return o, s_fin


def gla_state(k, v, log_a):
    bcum = jnp.cumsum(log_a, axis=1)
    k_d = k.astype(jnp.float32) * jnp.exp(bcum[:, -1:] - bcum)
    return jnp.einsum('bthk,bthv->bhkv', k_d, v.astype(jnp.float32))


def gla_split_cols(p):
    kt, vt, r = GLA_KEY_DIM, GLA_VAL_DIM, GLA_GATE_RANK
    return jnp.split(p, [kt, 2 * kt, 2 * kt + vt, 2 * kt + 2 * vt, 2 * kt + 2 * vt + r], axis=-1)


def gla_mixer(h, w_in, w_a2, b_a, head_gain, w_out, s0_f, s0_b):
    q, k, v, g, a_f, a_b = gla_split_cols(h @ w_in)
    q = heads(q, GLA_HEAD_K) * (GLA_HEAD_K ** -0.5)
    k = heads(k, GLA_HEAD_K)
    v = heads(v, GLA_HEAD_V)
    la_f = gla_log_decay(a_f, w_a2[0], b_a[0])
    la_b = gla_log_decay(a_b, w_a2[1], b_a[1])
    o_f, s_f = gla_scan(q, k, v, la_f, s0_f)
    flip = functools.partial(jnp.flip, axis=1)
    o_b, s_b = gla_scan(flip(q), flip(k), flip(v), flip(la_b), s0_b)
    o = o_f + flip(o_b)
    o = o * lax.rsqrt(jnp.mean(o * o, axis=-1, keepdims=True) + EPS)
    o = (o.astype(h.dtype) * head_gain).reshape(h.shape[0], h.shape[1], GLA_VAL_DIM)
    return (o * jax.nn.silu(g)) @ w_out, s_f, s_b


def gla_context_states(h, w_in, w_a2, b_a):
    kt, vt = GLA_KEY_DIM, GLA_VAL_DIM
    k, v = jnp.split(h @ w_in[:, kt:2 * kt + vt], [kt], axis=-1)
    a_f, a_b = jnp.split(h @ w_in[:, 2 * kt + 2 * vt:], 2, axis=-1)
    k = heads(k, GLA_HEAD_K)
    v = heads(v, GLA_HEAD_V)
    s_f = gla_state(k, v, gla_log_decay(a_f, w_a2[0], b_a[0]))
    s_b = gla_state(jnp.flip(k, 1), jnp.flip(v, 1), jnp.flip(gla_log_decay(a_b, w_a2[1], b_a[1]), 1))
    return s_f, s_b


def short_conv_mixer(h, w_in, conv_w, w_out, conv_fn):
    bg, cg, v = jnp.split(h @ w_in, 3, axis=-1)
    return (bg * conv_fn(cg * v, conv_w)) @ w_out


def conv_ffn(h, w_up, conv_w, conv_b, w_down, conv_fn):
    u = conv_fn(h @ w_up, conv_w) + conv_b
    a, gt = jnp.split(u, 2, axis=-1)
    return (a * jax.nn.silu(gt)) @ w_down


def setup_inputs(seed: int = 0) -> dict:
    key = jax.random.key(seed)
    ks = jax.random.split(key, 24)
    n_a = (DEPTH + N_MIXERS - 1) // N_MIXERS
    n_b = DEPTH // N_MIXERS
    f32 = jnp.float32

    def nrm(k, shape, scale):
        return jax.random.normal(k, shape, f32) * scale

    return {
        'x': nrm(ks[0], (BATCH, SEQ, D_MODEL), 1.0),
        'c': nrm(ks[1], (BATCH, D_MODEL), 1.0),
        'ctx': nrm(ks[2], (BATCH, CTX_LEN, D_MODEL), 1.0),
        'c_ctx': nrm(ks[3], (D_MODEL,), 1.0),
        'ada_w': nrm(ks[4], (DEPTH, D_MODEL, N_MOD * D_MODEL), 0.5 * D_MODEL ** -0.5),
        'ada_b': nrm(ks[5], (DEPTH, N_MOD * D_MODEL), 0.02),
        'norm_mix': 1.0 + nrm(ks[6], (DEPTH, D_MODEL), 0.02),
        'norm_ffn': 1.0 + nrm(ks[7], (DEPTH, D_MODEL), 0.02),
        'gla_w_in': nrm(ks[8], (n_a, D_MODEL, GLA_IN_DIM), D_MODEL ** -0.5),
        'gla_w_a2': nrm(ks[9], (n_a, 2, GLA_GATE_RANK, GLA_KEY_DIM), GLA_GATE_RANK ** -0.5),
        'gla_b_a': nrm(ks[10], (n_a, 2, GLA_KEY_DIM), 0.1),
        'gla_head_norm': 1.0 + nrm(ks[11], (n_a, GLA_HEAD_V), 0.02),
        'gla_w_out': nrm(ks[12], (n_a, GLA_VAL_DIM, D_MODEL), GLA_VAL_DIM ** -0.5),
        'sc_w_in': nrm(ks[13], (n_b, D_MODEL, 3 * SC_DIM), D_MODEL ** -0.5),
        'sc_conv_w': nrm(ks[14], (n_b, CONV_WIDTH, SC_DIM), CONV_WIDTH ** -0.5),
        'sc_w_out': nrm(ks[15], (n_b, SC_DIM, D_MODEL), SC_DIM ** -0.5),
        'ffn_w_up': nrm(ks[16], (DEPTH, D_MODEL, 2 * FFN_HIDDEN), D_MODEL ** -0.5),
        'ffn_conv_w': nrm(ks[17], (DEPTH, CONV_WIDTH, 2 * FFN_HIDDEN), CONV_WIDTH ** -0.5),
        'ffn_conv_b': nrm(ks[18], (DEPTH, 2 * FFN_HIDDEN), 0.02),
        'ffn_w_down': nrm(ks[19], (DEPTH, FFN_HIDDEN, D_MODEL), FFN_HIDDEN ** -0.5),
        'final_norm': 1.0 + nrm(ks[20], (D_MODEL,), 0.02),
    }


def reference(x, c, ctx, c_ctx, ada_w, ada_b, norm_mix, norm_ffn, gla_w_in, gla_w_a2, gla_b_a,
              gla_head_norm, gla_w_out, sc_w_in, sc_conv_w, sc_w_out, ffn_w_up, ffn_conv_w,
              ffn_conv_b, ffn_w_down, final_norm):
    rows = x.shape[1] // GRID_W
    conv_lat_rows = functools.partial(conv_grid, rows=rows, axis=2)
    conv_lat_cols = functools.partial(conv_grid, rows=rows, axis=1)
    h, hc = x, ctx
    sc, scc = jax.nn.silu(c), jax.nn.silu(c_ctx)
    for i in range(DEPTH):
        mixer, j = i % N_MIXERS, i // N_MIXERS
        ctx_later = any(l % N_MIXERS == 0 for l in range(i + 1, DEPTH))
        m = [t[:, None, :] for t in jnp.split(sc @ ada_w[i] + ada_b[i], N_MOD, axis=-1)]
        need_ctx = (mixer == 0) or ctx_later
        if need_ctx:
            mc = jnp.split(scc @ ada_w[i] + ada_b[i], N_MOD, axis=-1)
            hnc = modulate(hc, norm_mix[i], mc[0], mc[1])
        hn = modulate(h, norm_mix[i], m[0], m[1])
        if mixer == 0:
            if ctx_later:
                zero = jnp.zeros((hc.shape[0], GLA_HEADS, GLA_HEAD_K, GLA_HEAD_V), jnp.float32)
                yc, s_f, s_b = gla_mixer(hnc, gla_w_in[j], gla_w_a2[j], gla_b_a[j], gla_head_norm[j],
                                         gla_w_out[j], zero, zero)
            else:
                s_f, s_b = gla_context_states(hnc, gla_w_in[j], gla_w_a2[j], gla_b_a[j])
            y, _, _ = gla_mixer(hn, gla_w_in[j], gla_w_a2[j], gla_b_a[j], gla_head_norm[j],
                                gla_w_out[j], s_f, s_b)
        else:
            y = short_conv_mixer(hn, sc_w_in[j], sc_conv_w[j], sc_w_out[j], conv_lat_rows)
            if ctx_later:
                yc = short_conv_mixer(hnc, sc_w_in[j], sc_conv_w[j], sc_w_out[j], conv_seq)
        h = h + m[2] * y
        h = h + m[5] * conv_ffn(modulate(h, norm_ffn[i], m[3], m[4]), ffn_w_up[i], ffn_conv_w[i],
                                ffn_conv_b[i], ffn_w_down[i], conv_lat_cols)
        if ctx_later:
            hc = hc + mc[2] * yc
            hc = hc + mc[5] * conv_ffn(modulate(hc, norm_ffn[i], mc[3], mc[4]), ffn_w_up[i],
                                       ffn_conv_w[i], ffn_conv_b[i], ffn_w_down[i], conv_seq)
    return rmsnorm(h, final_norm)
```

```python
import functools

import jax
import jax.numpy as jnp
from jax import lax
from jax.experimental import pallas as pl
from jax.experimental.pallas import tpu as pltpu

F32 = jnp.float32
BF16 = jnp.bfloat16

EPS = 1e-6
GRID_W = 64
N_MOD = 6
GLA_HEADS = 4
GLA_GATE_RANK = 16
GLA_GATE_TAU = 16.0
CONV_WIDTH = 3

V7X_LANES = 128
V7X_MXU_DIM = 256
V7X_VMEM_BYTES = 64 * 1024 * 1024

GLA_CHUNK = 128
TOKEN_TILE = 512
FFN_COLS = 16
HID_CHUNK = 512

NT_DIMS = (((1,), (1,)), ((), ()))


def _vmem_limit(nbytes):
    return int(min(nbytes, V7X_VMEM_BYTES - 4 * 1024 * 1024))


def _resident(shape):
    return pl.BlockSpec(shape, lambda i, j: (0,) * len(shape), pipeline_mode=pl.Buffered(1))


def _modulated_norm(x, gain, shift, scale):
    y = x * lax.rsqrt(jnp.mean(x * x, axis=-1, keepdims=True) + EPS)
    return (y * gain) * (1.0 + scale) + shift


def _silu(x):
    return x * jax.nn.sigmoid(x)


def _ada_kernel(c_ref, w_ref, b_ref, o_ref):
    c = c_ref[...]
    sc = _silu(c).astype(BF16)
    o_ref[0] = jnp.dot(sc, w_ref[0].astype(BF16), preferred_element_type=F32) + b_ref[0]


def _ada_mod(cvec, ada_w, ada_b):
    depth, d, n = ada_w.shape
    rows = cvec.shape[0]
    tn = n // 4
    return pl.pallas_call(
        _ada_kernel,
        grid=(depth, n // tn),
        in_specs=[
            pl.BlockSpec((rows, d), lambda i, j: (0, 0)),
            pl.BlockSpec((1, d, tn), lambda i, j: (i, 0, j)),
            pl.BlockSpec((1, 1, tn), lambda i, j: (i, 0, j)),
        ],
        out_specs=pl.BlockSpec((1, rows, tn), lambda i, j: (i, 0, j)),
        out_shape=jax.ShapeDtypeStruct((depth, rows, n), F32),
        compiler_params=pltpu.CompilerParams(
            dimension_semantics=("parallel", "parallel"),
            vmem_limit_bytes=_vmem_limit(4 * d * tn * 4)),
        name="ada_mod",
    )(cvec, ada_w, ada_b.reshape(depth, 1, n))


def _proj_kernel(x_ref, shift_ref, scale_ref, gain_ref, w_ref, o_ref, *, n_chunk):
    hn = _modulated_norm(x_ref[0], gain_ref[...], shift_ref[0], scale_ref[0]).astype(BF16)
    n = w_ref.shape[1]
    for n0 in range(0, n, n_chunk):
        n1 = min(n0 + n_chunk, n)
        o_ref[0, :, n0:n1] = jnp.dot(hn, w_ref[:, n0:n1], preferred_element_type=F32).astype(o_ref.dtype)


def _modnorm_proj(x, mod, gain, w, *, shift_idx, scale_idx, tm):
    b, t, d = x.shape
    n = w.shape[1]
    per_batch = mod.shape[0] != 1
    mod_map = lambda idx: (lambda i, j: (i if per_batch else 0, 0, idx))
    n_chunk = 2 * V7X_MXU_DIM
    assert t % tm == 0 and n % V7X_MXU_DIM == 0
    return pl.pallas_call(
        functools.partial(_proj_kernel, n_chunk=n_chunk),
        grid=(b, t // tm),
        in_specs=[
            pl.BlockSpec((1, tm, d), lambda i, j: (i, j, 0)),
            pl.BlockSpec((1, 1, d), mod_map(shift_idx)),
            pl.BlockSpec((1, 1, d), mod_map(scale_idx)),
            _resident((1, d)),
            _resident((d, n)),
        ],
        out_specs=pl.BlockSpec((1, tm, n), lambda i, j: (i, j, 0)),
        out_shape=jax.ShapeDtypeStruct((b, t, n), BF16),
        compiler_params=pltpu.CompilerParams(
            dimension_semantics=("parallel", "parallel"),
            vmem_limit_bytes=_vmem_limit(2 * (tm * d * 4 + d * n * 2 + tm * n * 2) + 8 * tm * n_chunk * 4
                                         + 16 * 1024 * 1024)),
        name="modnorm_proj",
    )(x, mod, mod, gain.reshape(1, d), w)


def _gla_kernel(*refs, n_chunks, dk, dv, q_scale, states_only, zero_init):
    if states_only:
        k_ref, v_ref, a_ref, w2_ref, ba_ref, st_out_ref, pt_scr, et_scr = refs
    else:
        (q_ref, k_ref, v_ref, g_ref, a_ref, w2_ref, ba_ref, hg_ref, st0_ref, o_ref,
         qs_scr, qe_scr, ks_scr, pt_scr, et_scr, sst_scr) = refs
    c_len = GLA_CHUNK
    mid = c_len // 2

    row = lax.broadcasted_iota(jnp.int32, (c_len, c_len), 0)
    col = lax.broadcasted_iota(jnp.int32, (c_len, c_len), 1)
    lower = row >= col
    upper = row <= col
    tril = jnp.where(lower, 1.0, 0.0).astype(BF16)
    triu = jnp.where(upper, 1.0, 0.0).astype(BF16)

    def chunk_rows(c):
        return pl.ds(pl.multiple_of(c * c_len, c_len), c_len)

    def prepare(c, carry):
        rows = chunk_rows(c)
        z = jnp.dot(a_ref[0, rows, :], w2_ref[0], preferred_element_type=F32) + ba_ref[0]
        la = (jnp.minimum(z, 0.0) - jnp.log1p(jnp.exp(-jnp.abs(z)))) * (1.0 / GLA_GATE_TAU)
        hi = la.astype(BF16)
        lo = (la - hi.astype(F32)).astype(BF16)
        pf = jnp.dot(tril, jnp.concatenate([hi[:, :dk], lo[:, :dk]], axis=1), preferred_element_type=F32)
        pb = jnp.dot(triu, jnp.concatenate([hi[:, dk:], lo[:, dk:]], axis=1), preferred_element_type=F32)
        bf = pf[:, :dk] + pf[:, dk:]
        bb = pb[:, :dk] + pb[:, dk:]
        rf, tf = bf[mid - 1:mid, :], bf[c_len - 1:c_len, :]
        rb, tb = bb[mid:mid + 1, :], bb[0:1, :]
        k = k_ref[0, rows, :].astype(F32)
        ks_f = k * jnp.exp(rf - bf)
        ks_b = k * jnp.exp(rb - bb)
        kd = jnp.concatenate([ks_f * jnp.exp(tf - rf), ks_b * jnp.exp(tb - rb)], axis=1).astype(BF16)
        v_t = v_ref[0, rows, :].astype(F32).T.astype(BF16)
        pt_scr[c] = jnp.dot(v_t, kd, preferred_element_type=F32)
        et_scr[c] = jnp.concatenate([jnp.exp(tf), jnp.exp(tb)], axis=1)
        if not states_only:
            q = q_ref[0, rows, :].astype(F32) * q_scale
            qs_f = q * jnp.exp(bf - rf)
            qs_b = q * jnp.exp(bb - rb)
            qs_scr[rows, :] = jnp.concatenate([qs_f, qs_b], axis=1).astype(BF16)
            qe_scr[rows, :] = jnp.concatenate([qs_f * jnp.exp(rf), qs_b * jnp.exp(rb)], axis=1).astype(BF16)
            ks_scr[rows, :] = jnp.concatenate([ks_f, ks_b], axis=1).astype(BF16)
        return carry

    lax.fori_loop(0, n_chunks, prepare, 0)

    if zero_init:
        s0_f = jnp.zeros((dv, dk), F32)
        s0_b = jnp.zeros((dv, dk), F32)
    else:
        s0_f = st0_ref[0, 0, :, :dk]
        s0_b = st0_ref[0, 0, :, dk:]

    def fwd_step(c, s):
        if not states_only:
            sst_scr[c, :, :dk] = s.astype(BF16)
        return s * et_scr[c, :, :dk] + pt_scr[c, :, :dk]

    def bwd_step(i, s):
        c = n_chunks - 1 - i
        if not states_only:
            sst_scr[c, :, dk:] = s.astype(BF16)
        return s * et_scr[c, :, dk:] + pt_scr[c, :, dk:]

    s_f = lax.fori_loop(0, n_chunks, fwd_step, s0_f)
    s_b = lax.fori_loop(0, n_chunks, bwd_step, s0_b)

    if states_only:
        st_out_ref[0, 0, :, :dk] = s_f
        st_out_ref[0, 0, :, dk:] = s_b
        return

    def emit(c, carry):
        rows = chunk_rows(c)
        qs = qs_scr[rows, :]
        ks = ks_scr[rows, :]
        att_f = lax.dot_general(qs[:, :dk], ks[:, :dk], NT_DIMS, preferred_element_type=F32)
        att_b = lax.dot_general(qs[:, dk:], ks[:, dk:], NT_DIMS, preferred_element_type=F32)
        att = (jnp.where(lower, att_f, 0.0) + jnp.where(upper, att_b, 0.0)).astype(BF16)
        o = jnp.dot(att, v_ref[0, rows, :], preferred_element_type=F32)
        o = o + lax.dot_general(qe_scr[rows, :], sst_scr[c], NT_DIMS, preferred_element_type=F32)
        on = o * lax.rsqrt(jnp.mean(o * o, axis=-1, keepdims=True) + EPS)
        g = g_ref[0, rows, :].astype(F32)
        o_ref[0, rows, :] = ((on * hg_ref[...]) * _silu(g)).astype(o_ref.dtype)
        return carry

    lax.fori_loop(0, n_chunks, emit, 0)


def _gla_scan(proj, w2, ba, head_gain, st0, *, d_model, states_only):
    b, t, _ = proj.shape
    h = GLA_HEADS
    kt, vt = d_model // 2, d_model
    dk, dv = kt // h, vt // h
    assert t % GLA_CHUNK == 0 and dk == V7X_LANES and dv == V7X_MXU_DIM and 2 * dk == V7X_MXU_DIM
    n_chunks = t // GLA_CHUNK
    a_blk = (2 * kt + 2 * vt) // V7X_MXU_DIM
    q_spec = pl.BlockSpec((1, t, dk), lambda i, j: (i, 0, j))
    k_spec = pl.BlockSpec((1, t, dk), lambda i, j: (i, 0, kt // dk + j))
    v_spec = pl.BlockSpec((1, t, dv), lambda i, j: (i, 0, 2 * kt // dv + j))
    g_spec = pl.BlockSpec((1, t, dv), lambda i, j: (i, 0, (2 * kt + vt) // dv + j))
    a_spec = pl.BlockSpec((1, t, V7X_MXU_DIM), lambda i, j: (i, 0, a_blk))
    w2_spec = pl.BlockSpec((1, V7X_MXU_DIM, 2 * dk), lambda i, j: (j, 0, 0))
    ba_spec = pl.BlockSpec((1, 1, 2 * dk), lambda i, j: (j, 0, 0))
    st_spec = pl.BlockSpec((1, 1, dv, 2 * dk), lambda i, j: (i, j, 0, 0))
    common_scratch = [pltpu.VMEM((n_chunks, dv, 2 * dk), F32), pltpu.VMEM((n_chunks, 1, 2 * dk), F32)]
    kern = functools.partial(_gla_kernel, n_chunks=n_chunks, dk=dk, dv=dv, q_scale=float(dk) ** -0.5,
                             states_only=states_only, zero_init=states_only)
    if states_only:
        return pl.pallas_call(
            kern, grid=(b, h),
            in_specs=[k_spec, v_spec, a_spec, w2_spec, ba_spec],
            out_specs=st_spec,
            out_shape=jax.ShapeDtypeStruct((b, h, dv, 2 * dk), F32),
            scratch_shapes=common_scratch,
            compiler_params=pltpu.CompilerParams(
                dimension_semantics=("parallel", "parallel"),
                vmem_limit_bytes=_vmem_limit(32 * 1024 * 1024)),
            name="gla_ctx_states",
        )(proj, proj, proj, w2, ba)
    tok = lambda width: pltpu.VMEM((t, width), BF16)
    return pl.pallas_call(
        kern, grid=(b, h),
        in_specs=[q_spec, k_spec, v_spec, g_spec, a_spec, w2_spec, ba_spec,
                  pl.BlockSpec((1, dv), lambda i, j: (0, 0)), st_spec],
        out_specs=pl.BlockSpec((1, t, dv), lambda i, j: (i, 0, j)),
        out_shape=jax.ShapeDtypeStruct((b, t, vt), BF16),
        scratch_shapes=[tok(2 * dk), tok(2 * dk), tok(2 * dk)] + common_scratch
                       + [pltpu.VMEM((n_chunks, dv, 2 * dk), BF16)],
        compiler_params=pltpu.CompilerParams(
            dimension_semantics=("parallel", "parallel"),
            vmem_limit_bytes=_vmem_limit(48 * 1024 * 1024)),
        name="gla_scan",
    )(proj, proj, proj, proj, proj, w2, ba, head_gain.reshape(1, dv), st0)


def _out_proj_kernel(o_ref, x_ref, gate_ref, w_ref, h_ref):
    y = jnp.dot(o_ref[0], w_ref[...], preferred_element_type=F32)
    h_ref[0] = x_ref[0] + gate_ref[0] * y


def _out_proj(o, x, mod, w, *, gate_idx, tm):
    b, t, d = x.shape
    return pl.pallas_call(
        _out_proj_kernel,
        grid=(b, t // tm),
        in_specs=[
            pl.BlockSpec((1, tm, o.shape[2]), lambda i, j: (i, j, 0)),
            pl.BlockSpec((1, tm, d), lambda i, j: (i, j, 0)),
            pl.BlockSpec((1, 1, d), lambda i, j: (i, 0, gate_idx)),
            _resident(w.shape),
        ],
        out_specs=pl.BlockSpec((1, tm, d), lambda i, j: (i, j, 0)),
        out_shape=jax.ShapeDtypeStruct((b, t, d), F32),
        compiler_params=pltpu.CompilerParams(
            dimension_semantics=("parallel", "parallel"),
            vmem_limit_bytes=_vmem_limit(40 * 1024 * 1024)),
        name="gla_out_proj",
    )(o, x, mod, w)


def _shift_rows(u, k):
    z = jnp.zeros((abs(k), u.shape[1]), u.dtype)
    if k > 0:
        return jnp.concatenate([z, u[:-k]], axis=0)
    return jnp.concatenate([u[-k:], z], axis=0)


def _ffn_kernel(x_ref, shift_ref, scale_ref, gate_ref, gain_ref, wup_ref, cw_ref, cb_ref, wdn_ref,
                fin_ref, o_ref, *, hidden, final_norm):
    rows_img, cols, d = x_ref.shape[1:]
    x = x_ref[0].reshape(rows_img * cols, d)
    hn = _modulated_norm(x, gain_ref[...], shift_ref[0], scale_ref[0]).astype(BF16)

    def conv(u, c0):
        w = cw_ref[:, c0:c0 + HID_CHUNK]
        return (_shift_rows(u, cols) * w[0:1] + u * w[1:2] + _shift_rows(u, -cols) * w[2:3]
                + cb_ref[:, c0:c0 + HID_CHUNK])

    acc = jnp.zeros((rows_img * cols, d), F32)
    for c0 in range(0, hidden, HID_CHUNK):
        ua = jnp.dot(hn, wup_ref[:, c0:c0 + HID_CHUNK], preferred_element_type=F32)
        ug = jnp.dot(hn, wup_ref[:, hidden + c0:hidden + c0 + HID_CHUNK], preferred_element_type=F32)
        act = (conv(ua, c0) * _silu(conv(ug, hidden + c0))).astype(BF16)
        acc = acc + jnp.dot(act, wdn_ref[c0:c0 + HID_CHUNK, :], preferred_element_type=F32)
    out = x + gate_ref[0] * acc
    if final_norm:
        out = out * lax.rsqrt(jnp.mean(out * out, axis=-1, keepdims=True) + EPS) * fin_ref[...]
    o_ref[0] = out.reshape(rows_img, cols, d)


def _conv_ffn(x, mod, gain, w_up, conv_w, conv_b, w_down, fin_gain, *, mod_base, final_norm):
    b, t, d = x.shape
    hidden = w_down.shape[0]
    assert t % GRID_W == 0 and GRID_W % FFN_COLS == 0 and hidden % HID_CHUNK == 0
    rows_img = t // GRID_W
    x4 = x.reshape(b, rows_img, GRID_W, d)
    mod_spec = lambda idx: pl.BlockSpec((1, 1, d), lambda i, j: (i, 0, idx))
    const = _resident
    tile = pl.BlockSpec((1, rows_img, FFN_COLS, d), lambda i, j: (i, 0, j, 0))
    out = pl.pallas_call(
        functools.partial(_ffn_kernel, hidden=hidden, final_norm=final_norm),
        grid=(b, GRID_W // FFN_COLS),
        in_specs=[tile, mod_spec(mod_base), mod_spec(mod_base + 1), mod_spec(mod_base + 2),
                  const((1, d)), const(w_up.shape), const(conv_w.shape), const((1, 2 * hidden)),
                  const(w_down.shape), const((1, d))],
        out_specs=tile,
        out_shape=jax.ShapeDtypeStruct(x4.shape, F32),
        compiler_params=pltpu.CompilerParams(
            dimension_semantics=("parallel", "parallel"),
            vmem_limit_bytes=_vmem_limit(60 * 1024 * 1024)),
        name="conv_ffn",
    )(x4, mod, mod, mod, gain.reshape(1, d), w_up, conv_w, conv_b.reshape(1, 2 * hidden), w_down,
      fin_gain.reshape(1, d))
    return out.reshape(b, t, d)


def _short_conv_kernel(x_ref, shift_ref, scale_ref, gate_ref, gain_ref, win_ref, cw_ref, wout_ref,
                       o_ref, *, width):
    x = x_ref[0]
    tm = x.shape[0]
    hn = _modulated_norm(x, gain_ref[...], shift_ref[0], scale_ref[0]).astype(BF16)
    pos = lax.broadcasted_iota(jnp.int32, (tm, HID_CHUNK), 0) % GRID_W
    has_left = pos != 0
    has_right = pos != GRID_W - 1
    acc = jnp.zeros(x.shape, F32)
    for c0 in range(0, width, HID_CHUNK):
        proj = lambda base: jnp.dot(hn, win_ref[:, base + c0:base + c0 + HID_CHUNK],
                                    preferred_element_type=F32)
        bg, u = proj(0), proj(width) * proj(2 * width)
        w = cw_ref[:, c0:c0 + HID_CHUNK]
        left = jnp.where(has_left, pltpu.roll(u, 1, axis=0), 0.0)
        right = jnp.where(has_right, pltpu.roll(u, tm - 1, axis=0), 0.0)
        y = (bg * (left * w[0:1] + u * w[1:2] + right * w[2:3])).astype(BF16)
        acc = acc + jnp.dot(y, wout_ref[c0:c0 + HID_CHUNK, :], preferred_element_type=F32)
    o_ref[0] = x + gate_ref[0] * acc


def _short_conv(x, mod, gain, w_in, conv_w, w_out, *, tm):
    b, t, d = x.shape
    width = w_out.shape[0]
    assert t % tm == 0 and tm % GRID_W == 0 and width % HID_CHUNK == 0
    mod_spec = lambda idx: pl.BlockSpec((1, 1, d), lambda i, j: (i, 0, idx))
    const = _resident
    tile = pl.BlockSpec((1, tm, d), lambda i, j: (i, j, 0))
    return pl.pallas_call(
        functools.partial(_short_conv_kernel, width=width),
        grid=(b, t // tm),
        in_specs=[tile, mod_spec(0), mod_spec(1), mod_spec(2), const((1, d)), const(w_in.shape),
                  const(conv_w.shape), const(w_out.shape)],
        out_specs=tile,
        out_shape=jax.ShapeDtypeStruct(x.shape, F32),
        compiler_params=pltpu.CompilerParams(
            dimension_semantics=("parallel", "parallel"),
            vmem_limit_bytes=_vmem_limit(56 * 1024 * 1024)),
        name="short_conv",
    )(x, mod, mod, mod, gain.reshape(1, d), w_in, conv_w, w_out)


def kernel(x, c, ctx, c_ctx, ada_w, ada_b, norm_mix, norm_ffn, gla_w_in, gla_w_a2, gla_b_a,
           gla_head_norm, gla_w_out, sc_w_in, sc_conv_w, sc_w_out, ffn_w_up, ffn_conv_w,
           ffn_conv_b, ffn_w_down, final_norm):
    b, t, d = x.shape
    depth = ada_w.shape[0]
    assert depth == 2 and gla_w_in.shape[0] == 1 and sc_w_in.shape[0] == 1
    h = GLA_HEADS
    kt, vt, rank = d // 2, d, GLA_GATE_RANK
    dk = kt // h

    pad = (-(b + 1)) % 8
    cvec = jnp.concatenate([c, c_ctx[None, :], jnp.zeros((pad, d), F32)], axis=0)
    mods = _ada_mod(cvec, ada_w, ada_b)
    mod_x = [mods[i, :b].reshape(b, 1, N_MOD * d) for i in range(depth)]
    mod_ctx = mods[0, b:b + 1].reshape(1, 1, N_MOD * d)

    n_in = gla_w_in.shape[2]
    n_pad = (-n_in) % V7X_MXU_DIM
    w_in = jnp.pad(gla_w_in[0], ((0, 0), (0, n_pad))).astype(BF16)
    a2 = gla_w_a2[0].reshape(2, rank, h, dk)
    w2 = jnp.zeros((h, V7X_MXU_DIM, 2 * dk), F32)
    w2 = w2.at[:, :rank, :dk].set(a2[0].transpose(1, 0, 2))
    w2 = w2.at[:, rank:2 * rank, dk:].set(a2[1].transpose(1, 0, 2)).astype(BF16)
    ba = gla_b_a[0].reshape(2, h, dk).transpose(1, 0, 2).reshape(h, 1, 2 * dk)

    proj_ctx = _modnorm_proj(ctx, mod_ctx, norm_mix[0], w_in, shift_idx=0, scale_idx=1, tm=ctx.shape[1])
    st0 = _gla_scan(proj_ctx, w2, ba, gla_head_norm[0], None, d_model=d, states_only=True)
    proj_x = _modnorm_proj(x, mod_x[0], norm_mix[0], w_in, shift_idx=0, scale_idx=1, tm=TOKEN_TILE)
    o = _gla_scan(proj_x, w2, ba, gla_head_norm[0], st0, d_model=d, states_only=False)
    hcur = _out_proj(o, x, mod_x[0], gla_w_out[0].astype(BF16), gate_idx=2, tm=TOKEN_TILE)
    hcur = _conv_ffn(hcur, mod_x[0], norm_ffn[0], ffn_w_up[0].astype(BF16), ffn_conv_w[0], ffn_conv_b[0],
                     ffn_w_down[0].astype(BF16), final_norm, mod_base=3, final_norm=False)

    hcur = _short_conv(hcur, mod_x[1], norm_mix[1], sc_w_in[0].astype(BF16), sc_conv_w[0],
                       sc_w_out[0].astype(BF16), tm=TOKEN_TILE)
    return _conv_ffn(hcur, mod_x[1], norm_ffn[1], ffn_w_up[1].astype(BF16), ffn_conv_w[1], ffn_conv_b[1],
                     ffn_w_down[1].astype(BF16), final_norm, mod_base=3, final_norm=True)
```

```python
import functools

import jax
import jax.numpy as jnp
from jax import lax
from jax.experimental import pallas as pl
from jax.experimental.pallas import tpu as pltpu

F32 = jnp.float32
BF16 = jnp.bfloat16

EPS = 1e-6
GRID_W = 64
N_MOD = 6
GLA_HEADS = 4
GLA_GATE_RANK = 16
GLA_GATE_TAU = 16.0
CONV_WIDTH = 3

V7X_LANES = 128
V7X_MXU_DIM = 256
V7X_VMEM_BYTES = 64 * 1024 * 1024

GLA_CHUNK = 128
TOKEN_TILE = 512
FFN_COLS = 16
HID_CHUNK = 512

NT_DIMS = (((1,), (1,)), ((), ()))
TN_DIMS = (((0,), (0,)), ((), ()))


def _vmem_limit(nbytes):
    return int(min(nbytes, V7X_VMEM_BYTES - 4 * 1024 * 1024))


def _resident(shape):
    return pl.BlockSpec(shape, lambda i, j: (0,) * len(shape), pipeline_mode=pl.Buffered(1))


def _modulated_norm(x, gain, shift, scale):
    y = x * lax.rsqrt(jnp.mean(x * x, axis=-1, keepdims=True) + EPS)
    return (y * gain) * (1.0 + scale) + shift


def _silu(x):
    return x * jax.nn.sigmoid(x)


def _ada_kernel(c_ref, w_ref, b_ref, o_ref):
    c = c_ref[...]
    sc = _silu(c).astype(BF16)
    o_ref[0] = jnp.dot(sc, w_ref[0].astype(BF16), preferred_element_type=F32) + b_ref[0]


def _ada_mod(cvec, ada_w, ada_b):
    depth, d, n = ada_w.shape
    rows = cvec.shape[0]
    tn = n // 4
    return pl.pallas_call(
        _ada_kernel,
        grid=(depth, n // tn),
        in_specs=[
            pl.BlockSpec((rows, d), lambda i, j: (0, 0)),
            pl.BlockSpec((1, d, tn), lambda i, j: (i, 0, j)),
            pl.BlockSpec((1, 1, tn), lambda i, j: (i, 0, j)),
        ],
        out_specs=pl.BlockSpec((1, rows, tn), lambda i, j: (i, 0, j)),
        out_shape=jax.ShapeDtypeStruct((depth, rows, n), F32),
        compiler_params=pltpu.CompilerParams(
            dimension_semantics=("parallel", "parallel"),
            vmem_limit_bytes=_vmem_limit(4 * d * tn * 4)),
        name="ada_mod",
    )(cvec, ada_w, ada_b.reshape(depth, 1, n))


def _proj_kernel(x_ref, shift_ref, scale_ref, gain_ref, w_ref, o_ref, *, n_chunk):
    hn = _modulated_norm(x_ref[0], gain_ref[...], shift_ref[0], scale_ref[0]).astype(BF16)
    n = w_ref.shape[1]
    for n0 in range(0, n, n_chunk):
        n1 = min(n0 + n_chunk, n)
        o_ref[0, :, n0:n1] = jnp.dot(hn, w_ref[:, n0:n1], preferred_element_type=F32).astype(o_ref.dtype)


def _modnorm_proj(x, mod, gain, w, *, shift_idx, scale_idx, tm):
    b, t, d = x.shape
    n = w.shape[1]
    per_batch = mod.shape[0] != 1
    mod_map = lambda idx: (lambda i, j: (i if per_batch else 0, 0, idx))
    n_chunk = 2 * V7X_MXU_DIM
    assert t % tm == 0 and n % V7X_MXU_DIM == 0
    return pl.pallas_call(
        functools.partial(_proj_kernel, n_chunk=n_chunk),
        grid=(b, t // tm),
        in_specs=[
            pl.BlockSpec((1, tm, d), lambda i, j: (i, j, 0)),
            pl.BlockSpec((1, 1, d), mod_map(shift_idx)),
            pl.BlockSpec((1, 1, d), mod_map(scale_idx)),
            _resident((1, d)),
            _resident((d, n)),
        ],
        out_specs=pl.BlockSpec((1, tm, n), lambda i, j: (i, j, 0)),
        out_shape=jax.ShapeDtypeStruct((b, t, n), BF16),
        compiler_params=pltpu.CompilerParams(
            dimension_semantics=("parallel", "parallel"),
            vmem_limit_bytes=_vmem_limit(2 * (tm * d * 4 + d * n * 2 + tm * n * 2) + 8 * tm * n_chunk * 4
                                         + 16 * 1024 * 1024)),
        name="modnorm_proj",
    )(x, mod, mod, gain.reshape(1, d), w)


def _gla_kernel(*refs, n_chunks, dk, dv, q_scale, states_only, zero_init):
    if states_only:
        k_ref, v_ref, a_ref, w2_ref, ba_ref, st_out_ref, pt_scr, et_scr = refs
    else:
        (q_ref, k_ref, v_ref, g_ref, a_ref, w2_ref, ba_ref, hg_ref, st0_ref, o_ref,
         qs_scr, qe_scr, ks_scr, pt_scr, et_scr, sst_scr) = refs
    c_len = GLA_CHUNK
    mid = c_len // 2

    row = lax.broadcasted_iota(jnp.int32, (c_len, c_len), 0)
    col = lax.broadcasted_iota(jnp.int32, (c_len, c_len), 1)
    lower = row >= col
    upper = row <= col
    tril = jnp.where(lower, 1.0, 0.0).astype(BF16)
    triu = jnp.where(upper, 1.0, 0.0).astype(BF16)

    def rows_of(i):
        return slice(i * c_len, (i + 1) * c_len)

    def log_decay(i, _):
        z = jnp.dot(a_ref[0, rows_of(i), :], w2_ref[0], preferred_element_type=F32) + ba_ref[0]
        la = (jnp.minimum(z, 0.0) - jnp.log(1.0 + jnp.exp(-jnp.abs(z)))) * (1.0 / GLA_GATE_TAU)
        hi = la.astype(BF16)
        lo = (la - hi.astype(F32)).astype(BF16)
        return (jnp.concatenate([hi[:, :dk], lo[:, :dk]], axis=1),
                jnp.concatenate([hi[:, dk:], lo[:, dk:]], axis=1))

    def scaled_operands(i, split):
        rows = rows_of(i)
        pf = jnp.dot(tril, split[0], preferred_element_type=F32)
        pb = jnp.dot(triu, split[1], preferred_element_type=F32)
        bf = pf[:, :dk] + pf[:, dk:]
        bb = pb[:, :dk] + pb[:, dk:]
        rf, tf = bf[mid - 1:mid, :], bf[c_len - 1:c_len, :]
        rb, tb = bb[mid:mid + 1, :], bb[0:1, :]
        k = k_ref[0, rows, :].astype(F32)
        dec_f = jnp.exp(bf - rf)
        dec_b = jnp.exp(bb - rb)
        ks_f = k * (1.0 / dec_f)
        ks_b = k * (1.0 / dec_b)
        et_scr[i] = jnp.concatenate([jnp.exp(tf), jnp.exp(tb)], axis=1)
        if not states_only:
            q = q_ref[0, rows, :].astype(F32) * q_scale
            qs_f = q * dec_f
            qs_b = q * dec_b
            qs_scr[rows, :] = jnp.concatenate([qs_f, qs_b], axis=1).astype(BF16)
            qe_scr[rows, :] = jnp.concatenate([qs_f * jnp.exp(rf), qs_b * jnp.exp(rb)], axis=1).astype(BF16)
            ks_scr[rows, :] = jnp.concatenate([ks_f, ks_b], axis=1).astype(BF16)
        return jnp.concatenate([ks_f * jnp.exp(tf - rf), ks_b * jnp.exp(tb - rb)], axis=1).astype(BF16)

    def state_increment(i, kd):
        pt_scr[i] = lax.dot_general(v_ref[0, rows_of(i), :], kd, TN_DIMS, preferred_element_type=F32)

    _run_staggered([log_decay, scaled_operands, state_increment], n_chunks)

    if zero_init:
        s0_f = jnp.zeros((dv, dk), F32)
        s0_b = jnp.zeros((dv, dk), F32)
    else:
        s0_f = st0_ref[0, 0, :, :dk]
        s0_b = st0_ref[0, 0, :, dk:]

    def fwd_step(c, s):
        if not states_only:
            sst_scr[c, :, :dk] = s.astype(BF16)
        return s * et_scr[c, :, :dk] + pt_scr[c, :, :dk]

    def bwd_step(i, s):
        c = n_chunks - 1 - i
        if not states_only:
            sst_scr[c, :, dk:] = s.astype(BF16)
        return s * et_scr[c, :, dk:] + pt_scr[c, :, dk:]

    s_f = lax.fori_loop(0, n_chunks, fwd_step, s0_f)
    s_b = lax.fori_loop(0, n_chunks, bwd_step, s0_b)

    if states_only:
        st_out_ref[0, 0, :, :dk] = s_f
        st_out_ref[0, 0, :, dk:] = s_b
        return

    def intra_scores(i, _):
        qs = qs_scr[rows_of(i), :]
        ks = ks_scr[rows_of(i), :]
        att_f = lax.dot_general(qs[:, :dk], ks[:, :dk], NT_DIMS, preferred_element_type=F32)
        att_b = lax.dot_general(qs[:, dk:], ks[:, dk:], NT_DIMS, preferred_element_type=F32)
        return (jnp.where(lower, att_f, 0.0) + jnp.where(upper, att_b, 0.0)).astype(BF16)

    def outputs(i, att):
        rows = rows_of(i)
        o = jnp.dot(att, v_ref[0, rows, :], preferred_element_type=F32)
        o = o + lax.dot_general(qe_scr[rows, :], sst_scr[i], NT_DIMS, preferred_element_type=F32)
        on = o * lax.rsqrt(jnp.mean(o * o, axis=-1, keepdims=True) + EPS)
        g = g_ref[0, rows, :].astype(F32)
        o_ref[0, rows, :] = ((on * hg_ref[...]) * _silu(g)).astype(o_ref.dtype)

    _run_staggered([intra_scores, outputs], n_chunks)


def _run_staggered(stages, n_items):
    handoff = [dict() for _ in stages]
    for t in range(n_items + len(stages) - 1):
        for s in reversed(range(len(stages))):
            i = t - s
            if 0 <= i < n_items:
                arg = handoff[s - 1].pop(i) if s else None
                handoff[s][i] = stages[s](i, arg)


def _gla_scan(proj, w2, ba, head_gain, st0, *, d_model, states_only):
    b, t, _ = proj.shape
    h = GLA_HEADS
    kt, vt = d_model // 2, d_model
    dk, dv = kt // h, vt // h
    assert t % GLA_CHUNK == 0 and dk == V7X_LANES and dv == V7X_MXU_DIM and 2 * dk == V7X_MXU_DIM
    n_chunks = t // GLA_CHUNK
    a_blk = (2 * kt + 2 * vt) // V7X_MXU_DIM
    q_spec = pl.BlockSpec((1, t, dk), lambda i, j: (i, 0, j))
    k_spec = pl.BlockSpec((1, t, dk), lambda i, j: (i, 0, kt // dk + j))
    v_spec = pl.BlockSpec((1, t, dv), lambda i, j: (i, 0, 2 * kt // dv + j))
    g_spec = pl.BlockSpec((1, t, dv), lambda i, j: (i, 0, (2 * kt + vt) // dv + j))
    a_spec = pl.BlockSpec((1, t, V7X_MXU_DIM), lambda i, j: (i, 0, a_blk))
    w2_spec = pl.BlockSpec((1, V7X_MXU_DIM, 2 * dk), lambda i, j: (j, 0, 0))
    ba_spec = pl.BlockSpec((1, 1, 2 * dk), lambda i, j: (j, 0, 0))
    st_spec = pl.BlockSpec((1, 1, dv, 2 * dk), lambda i, j: (i, j, 0, 0))
    common_scratch = [pltpu.VMEM((n_chunks, dv, 2 * dk), F32), pltpu.VMEM((n_chunks, 1, 2 * dk), F32)]
    kern = functools.partial(_gla_kernel, n_chunks=n_chunks, dk=dk, dv=dv, q_scale=float(dk) ** -0.5,
                             states_only=states_only, zero_init=states_only)
    if states_only:
        return pl.pallas_call(
            kern, grid=(b, h),
            in_specs=[k_spec, v_spec, a_spec, w2_spec, ba_spec],
            out_specs=st_spec,
            out_shape=jax.ShapeDtypeStruct((b, h, dv, 2 * dk), F32),
            scratch_shapes=common_scratch,
            compiler_params=pltpu.CompilerParams(
                dimension_semantics=("parallel", "parallel"),
                vmem_limit_bytes=_vmem_limit(32 * 1024 * 1024)),
            name="gla_ctx_states",
        )(proj, proj, proj, w2, ba)
    tok = lambda width: pltpu.VMEM((t, width), BF16)
    return pl.pallas_call(
        kern, grid=(b, h),
        in_specs=[q_spec, k_spec, v_spec, g_spec, a_spec, w2_spec, ba_spec,
                  pl.BlockSpec((1, dv), lambda i, j: (0, 0)), st_spec],
        out_specs=pl.BlockSpec((1, t, dv), lambda i, j: (i, 0, j)),
        out_shape=jax.ShapeDtypeStruct((b, t, vt), BF16),
        scratch_shapes=[tok(2 * dk), tok(2 * dk), tok(2 * dk)] + common_scratch
                       + [pltpu.VMEM((n_chunks, dv, 2 * dk), BF16)],
        compiler_params=pltpu.CompilerParams(
            dimension_semantics=("parallel", "parallel"),
            vmem_limit_bytes=_vmem_limit(48 * 1024 * 1024)),
        name="gla_scan",
    )(proj, proj, proj, proj, proj, w2, ba, head_gain.reshape(1, dv), st0)


def _out_proj_kernel(o_ref, x_ref, gate_ref, w_ref, h_ref):
    y = jnp.dot(o_ref[0], w_ref[...], preferred_element_type=F32)
    h_ref[0] = x_ref[0] + gate_ref[0] * y


def _out_proj(o, x, mod, w, *, gate_idx, tm):
    b, t, d = x.shape
    return pl.pallas_call(
        _out_proj_kernel,
        grid=(b, t // tm),
        in_specs=[
            pl.BlockSpec((1, tm, o.shape[2]), lambda i, j: (i, j, 0)),
            pl.BlockSpec((1, tm, d), lambda i, j: (i, j, 0)),
            pl.BlockSpec((1, 1, d), lambda i, j: (i, 0, gate_idx)),
            _resident(w.shape),
        ],
        out_specs=pl.BlockSpec((1, tm, d), lambda i, j: (i, j, 0)),
        out_shape=jax.ShapeDtypeStruct((b, t, d), F32),
        compiler_params=pltpu.CompilerParams(
            dimension_semantics=("parallel", "parallel"),
            vmem_limit_bytes=_vmem_limit(40 * 1024 * 1024)),
        name="gla_out_proj",
    )(o, x, mod, w)


def _shift_rows(u, k):
    z = jnp.zeros((abs(k), u.shape[1]), u.dtype)
    if k > 0:
        return jnp.concatenate([z, u[:-k]], axis=0)
    return jnp.concatenate([u[-k:], z], axis=0)


def _ffn_kernel(x_ref, shift_ref, scale_ref, gate_ref, gain_ref, wup_ref, cw_ref, cb_ref, wdn_ref,
                fin_ref, o_ref, *, hidden, final_norm):
    rows_img, cols, d = x_ref.shape[1:]
    x = x_ref[0].reshape(rows_img * cols, d)
    hn = _modulated_norm(x, gain_ref[...], shift_ref[0], scale_ref[0]).astype(BF16)

    def conv(u, c0):
        w = cw_ref[:, c0:c0 + HID_CHUNK]
        return (_shift_rows(u, cols) * w[0:1] + u * w[1:2] + _shift_rows(u, -cols) * w[2:3]
                + cb_ref[:, c0:c0 + HID_CHUNK])

    acc = jnp.zeros((rows_img * cols, d), F32)
    for c0 in range(0, hidden, HID_CHUNK):
        ua = jnp.dot(hn, wup_ref[:, c0:c0 + HID_CHUNK], preferred_element_type=F32)
        ug = jnp.dot(hn, wup_ref[:, hidden + c0:hidden + c0 + HID_CHUNK], preferred_element_type=F32)
        act = (conv(ua, c0) * _silu(conv(ug, hidden + c0))).astype(BF16)
        acc = acc + jnp.dot(act, wdn_ref[c0:c0 + HID_CHUNK, :], preferred_element_type=F32)
    out = x + gate_ref[0] * acc
    if final_norm:
        out = out * lax.rsqrt(jnp.mean(out * out, axis=-1, keepdims=True) + EPS) * fin_ref[...]
    o_ref[0] = out.reshape(rows_img, cols, d)


def _conv_ffn(x, mod, gain, w_up, conv_w, conv_b, w_down, fin_gain, *, mod_base, final_norm):
    b, t, d = x.shape
    hidden = w_down.shape[0]
    assert t % GRID_W == 0 and GRID_W % FFN_COLS == 0 and hidden % HID_CHUNK == 0
    rows_img = t // GRID_W
    x4 = x.reshape(b, rows_img, GRID_W, d)
    mod_spec = lambda idx: pl.BlockSpec((1, 1, d), lambda i, j: (i, 0, idx))
    const = _resident
    tile = pl.BlockSpec((1, rows_img, FFN_COLS, d), lambda i, j: (i, 0, j, 0))
    out = pl.pallas_call(
        functools.partial(_ffn_kernel, hidden=hidden, final_norm=final_norm),
        grid=(b, GRID_W // FFN_COLS),
        in_specs=[tile, mod_spec(mod_base), mod_spec(mod_base + 1), mod_spec(mod_base + 2),
                  const((1, d)), const(w_up.shape), const(conv_w.shape), const((1, 2 * hidden)),
                  const(w_down.shape), const((1, d))],
        out_specs=tile,
        out_shape=jax.ShapeDtypeStruct(x4.shape, F32),
        compiler_params=pltpu.CompilerParams(
            dimension_semantics=("parallel", "parallel"),
            vmem_limit_bytes=_vmem_limit(60 * 1024 * 1024)),
        name="conv_ffn",
    )(x4, mod, mod, mod, gain.reshape(1, d), w_up, conv_w, conv_b.reshape(1, 2 * hidden), w_down,
      fin_gain.reshape(1, d))
    return out.reshape(b, t, d)


def _short_conv_kernel(x_ref, shift_ref, scale_ref, gate_ref, gain_ref, win_ref, cw_ref, wout_ref,
                       o_ref, *, width):
    x = x_ref[0]
    tm = x.shape[0]
    hn = _modulated_norm(x, gain_ref[...], shift_ref[0], scale_ref[0]).astype(BF16)
    pos = lax.broadcasted_iota(jnp.int32, (tm, HID_CHUNK), 0) % GRID_W
    has_left = pos != 0
    has_right = pos != GRID_W - 1
    acc = jnp.zeros(x.shape, F32)
    for c0 in range(0, width, HID_CHUNK):
        proj = lambda base: jnp.dot(hn, win_ref[:, base + c0:base + c0 + HID_CHUNK],
                                    preferred_element_type=F32)
        bg, u = proj(0), proj(width) * proj(2 * width)
        w = cw_ref[:, c0:c0 + HID_CHUNK]
        left = jnp.where(has_left, pltpu.roll(u, 1, axis=0), 0.0)
        right = jnp.where(has_right, pltpu.roll(u, tm - 1, axis=0), 0.0)
        y = (bg * (left * w[0:1] + u * w[1:2] + right * w[2:3])).astype(BF16)
        acc = acc + jnp.dot(y, wout_ref[c0:c0 + HID_CHUNK, :], preferred_element_type=F32)
    o_ref[0] = x + gate_ref[0] * acc


def _short_conv(x, mod, gain, w_in, conv_w, w_out, *, tm):
    b, t, d = x.shape
    width = w_out.shape[0]
    assert t % tm == 0 and tm % GRID_W == 0 and width % HID_CHUNK == 0
    mod_spec = lambda idx: pl.BlockSpec((1, 1, d), lambda i, j: (i, 0, idx))
    const = _resident
    tile = pl.BlockSpec((1, tm, d), lambda i, j: (i, j, 0))
    return pl.pallas_call(
        functools.partial(_short_conv_kernel, width=width),
        grid=(b, t // tm),
        in_specs=[tile, mod_spec(0), mod_spec(1), mod_spec(2), const((1, d)), const(w_in.shape),
                  const(conv_w.shape), const(w_out.shape)],
        out_specs=tile,
        out_shape=jax.ShapeDtypeStruct(x.shape, F32),
        compiler_params=pltpu.CompilerParams(
            dimension_semantics=("parallel", "parallel"),
            vmem_limit_bytes=_vmem_limit(56 * 1024 * 1024)),
        name="short_conv",
    )(x, mod, mod, mod, gain.reshape(1, d), w_in, conv_w, w_out)


def kernel(x, c, ctx, c_ctx, ada_w, ada_b, norm_mix, norm_ffn, gla_w_in, gla_w_a2, gla_b_a,
           gla_head_norm, gla_w_out, sc_w_in, sc_conv_w, sc_w_out, ffn_w_up, ffn_conv_w,
           ffn_conv_b, ffn_w_down, final_norm):
    b, t, d = x.shape
    depth = ada_w.shape[0]
    assert depth == 2 and gla_w_in.shape[0] == 1 and sc_w_in.shape[0] == 1
    h = GLA_HEADS
    kt, vt, rank = d // 2, d, GLA_GATE_RANK
    dk = kt // h

    pad = (-(b + 1)) % 8
    cvec = jnp.concatenate([c, c_ctx[None, :], jnp.zeros((pad, d), F32)], axis=0)
    mods = _ada_mod(cvec, ada_w, ada_b)
    mod_x = [mods[i, :b].reshape(b, 1, N_MOD * d) for i in range(depth)]
    mod_ctx = mods[0, b:b + 1].reshape(1, 1, N_MOD * d)

    n_in = gla_w_in.shape[2]
    n_pad = (-n_in) % V7X_MXU_DIM
    w_in = jnp.pad(gla_w_in[0], ((0, 0), (0, n_pad))).astype(BF16)
    a2 = gla_w_a2[0].reshape(2, rank, h, dk)
    w2 = jnp.zeros((h, V7X_MXU_DIM, 2 * dk), F32)
    w2 = w2.at[:, :rank, :dk].set(a2[0].transpose(1, 0, 2))
    w2 = w2.at[:, rank:2 * rank, dk:].set(a2[1].transpose(1, 0, 2)).astype(BF16)
    ba = gla_b_a[0].reshape(2, h, dk).transpose(1, 0, 2).reshape(h, 1, 2 * dk)

    proj_ctx = _modnorm_proj(ctx, mod_ctx, norm_mix[0], w_in, shift_idx=0, scale_idx=1, tm=ctx.shape[1])
    st0 = _gla_scan(proj_ctx, w2, ba, gla_head_norm[0], None, d_model=d, states_only=True)
    proj_x = _modnorm_proj(x, mod_x[0], norm_mix[0], w_in, shift_idx=0, scale_idx=1, tm=TOKEN_TILE)
    o = _gla_scan(proj_x, w2, ba, gla_head_norm[0], st0, d_model=d, states_only=False)
    hcur = _out_proj(o, x, mod_x[0], gla_w_out[0].astype(BF16), gate_idx=2, tm=TOKEN_TILE)
    hcur = _conv_ffn(hcur, mod_x[0], norm_ffn[0], ffn_w_up[0].astype(BF16), ffn_conv_w[0], ffn_conv_b[0],
                     ffn_w_down[0].astype(BF16), final_norm, mod_base=3, final_norm=False)

    hcur = _short_conv(hcur, mod_x[1], norm_mix[1], sc_w_in[0].astype(BF16), sc_conv_w[0],
                       sc_w_out[0].astype(BF16), tm=TOKEN_TILE)
    return _conv_ffn(hcur, mod_x[1], norm_ffn[1], ffn_w_up[1].astype(BF16), ffn_conv_w[1], ffn_conv_b[1],
                     ffn_w_down[1].astype(BF16), final_norm, mod_base=3, final_norm=True)
```

```python
import functools

import jax
import jax.numpy as jnp
from jax import lax
from jax.experimental import pallas as pl
from jax.experimental.pallas import tpu as pltpu

F32 = jnp.float32
BF16 = jnp.bfloat16

EPS = 1e-6
GRID_W = 64
N_MOD = 6
GLA_HEADS = 4
GLA_GATE_RANK = 16
GLA_GATE_TAU = 16.0
CONV_WIDTH = 3

V7X_LANES = 128
V7X_MXU_DIM = 256
V7X_VMEM_BYTES = 64 * 1024 * 1024

GLA_CHUNK = 128
TOKEN_TILE = 512
FFN_COLS = 16
HID_CHUNK = 512
SC_CHUNK = 256

NT_DIMS = (((1,), (1,)), ((), ()))
TN_DIMS = (((0,), (0,)), ((), ()))


def _vmem_limit(nbytes):
    return int(min(nbytes, V7X_VMEM_BYTES - 4 * 1024 * 1024))


def _resident(shape):
    return pl.BlockSpec(shape, lambda i, j: (0,) * len(shape), pipeline_mode=pl.Buffered(1))


def _modulated_norm(x, gain, shift, scale):
    y = x * lax.rsqrt(jnp.mean(x * x, axis=-1, keepdims=True) + EPS)
    return y * (gain * (1.0 + scale)) + shift


def _silu(x):
    return x * jax.nn.sigmoid(x)


def _ada_kernel(c_ref, w_ref, b_ref, o_ref):
    c = c_ref[...]
    sc = _silu(c).astype(BF16)
    o_ref[0] = jnp.dot(sc, w_ref[0].astype(BF16), preferred_element_type=F32) + b_ref[0]


def _ada_mod(cvec, ada_w, ada_b):
    depth, d, n = ada_w.shape
    rows = cvec.shape[0]
    tn = n // 4
    return pl.pallas_call(
        _ada_kernel,
        grid=(depth, n // tn),
        in_specs=[
            pl.BlockSpec((rows, d), lambda i, j: (0, 0)),
            pl.BlockSpec((1, d, tn), lambda i, j: (i, 0, j)),
            pl.BlockSpec((1, 1, tn), lambda i, j: (i, 0, j)),
        ],
        out_specs=pl.BlockSpec((1, rows, tn), lambda i, j: (i, 0, j)),
        out_shape=jax.ShapeDtypeStruct((depth, rows, n), F32),
        compiler_params=pltpu.CompilerParams(
            dimension_semantics=("parallel", "parallel"),
            vmem_limit_bytes=_vmem_limit(4 * d * tn * 4)),
        name="ada_mod",
    )(cvec, ada_w, ada_b.reshape(depth, 1, n))


def _proj_kernel(x_ref, shift_ref, scale_ref, gain_ref, w_ref, o_ref, *, n_chunk):
    hn = _modulated_norm(x_ref[0], gain_ref[...], shift_ref[0], scale_ref[0]).astype(BF16)
    n = w_ref.shape[1]
    for n0 in range(0, n, n_chunk):
        n1 = min(n0 + n_chunk, n)
        o_ref[0, :, n0:n1] = jnp.dot(hn, w_ref[:, n0:n1], preferred_element_type=F32).astype(o_ref.dtype)


def _modnorm_proj(x, mod, gain, w, *, shift_idx, scale_idx, tm):
    b, t, d = x.shape
    n = w.shape[1]
    per_batch = mod.shape[0] != 1
    mod_map = lambda idx: (lambda i, j: (i if per_batch else 0, 0, idx))
    n_chunk = 2 * V7X_MXU_DIM
    assert t % tm == 0 and n % V7X_MXU_DIM == 0
    return pl.pallas_call(
        functools.partial(_proj_kernel, n_chunk=n_chunk),
        grid=(b, t // tm),
        in_specs=[
            pl.BlockSpec((1, tm, d), lambda i, j: (i, j, 0)),
            pl.BlockSpec((1, 1, d), mod_map(shift_idx)),
            pl.BlockSpec((1, 1, d), mod_map(scale_idx)),
            _resident((1, d)),
            _resident((d, n)),
        ],
        out_specs=pl.BlockSpec((1, tm, n), lambda i, j: (i, j, 0)),
        out_shape=jax.ShapeDtypeStruct((b, t, n), BF16),
        compiler_params=pltpu.CompilerParams(
            dimension_semantics=("parallel", "parallel"),
            vmem_limit_bytes=_vmem_limit(2 * (tm * d * 4 + d * n * 2 + tm * n * 2) + 8 * tm * n_chunk * 4
                                         + 16 * 1024 * 1024)),
        name="modnorm_proj",
    )(x, mod, mod, gain.reshape(1, d), w)


def _gla_kernel(*refs, n_chunks, dk, dv, q_scale, states_only, zero_init):
    if states_only:
        k_ref, v_ref, a_ref, w2_ref, ba_ref, st_out_ref, pt_scr, et_scr = refs
    else:
        (q_ref, k_ref, v_ref, g_ref, a_ref, w2_ref, ba_ref, hg_ref, st0_ref, o_ref,
         qs_scr, qe_scr, ks_scr, pt_scr, et_scr, sst_scr) = refs
    c_len = GLA_CHUNK
    mid = c_len // 2

    row = lax.broadcasted_iota(jnp.int32, (c_len, c_len), 0)
    col = lax.broadcasted_iota(jnp.int32, (c_len, c_len), 1)
    lower = row >= col
    upper = row <= col
    tril = jnp.where(lower, 1.0, 0.0).astype(BF16)
    triu = jnp.where(upper, 1.0, 0.0).astype(BF16)

    def rows_of(i):
        return slice(i * c_len, (i + 1) * c_len)

    def log_decay(i, _):
        z = jnp.dot(a_ref[0, rows_of(i), :], w2_ref[0], preferred_element_type=F32) + ba_ref[0]
        la = (jnp.minimum(z, 0.0) - jnp.log(1.0 + jnp.exp(-jnp.abs(z)))) * (1.0 / GLA_GATE_TAU)
        hi = la.astype(BF16)
        lo = (la - hi.astype(F32)).astype(BF16)
        return (jnp.concatenate([hi[:, :dk], lo[:, :dk]], axis=1),
                jnp.concatenate([hi[:, dk:], lo[:, dk:]], axis=1))

    def scaled_operands(i, split):
        rows = rows_of(i)
        pf = jnp.dot(tril, split[0], preferred_element_type=F32)
        pb = jnp.dot(triu, split[1], preferred_element_type=F32)
        bf = pf[:, :dk] + pf[:, dk:]
        bb = pb[:, :dk] + pb[:, dk:]
        rf, tf = bf[mid - 1:mid, :], bf[c_len - 1:c_len, :]
        rb, tb = bb[mid:mid + 1, :], bb[0:1, :]
        k = k_ref[0, rows, :].astype(F32)
        dec_f = jnp.exp(bf - rf)
        dec_b = jnp.exp(bb - rb)
        ks_f = k * (1.0 / dec_f)
        ks_b = k * (1.0 / dec_b)
        et_scr[i] = jnp.concatenate([jnp.exp(tf), jnp.exp(tb)], axis=1)
        if not states_only:
            q = q_ref[0, rows, :].astype(F32) * q_scale
            qs_f = q * dec_f
            qs_b = q * dec_b
            qs_scr[rows, :] = jnp.concatenate([qs_f, qs_b], axis=1).astype(BF16)
            qe_scr[rows, :] = jnp.concatenate([qs_f * jnp.exp(rf), qs_b * jnp.exp(rb)], axis=1).astype(BF16)
            ks_scr[rows, :] = jnp.concatenate([ks_f, ks_b], axis=1).astype(BF16)
        return jnp.concatenate([ks_f * jnp.exp(tf - rf), ks_b * jnp.exp(tb - rb)], axis=1).astype(BF16)

    def state_increment(i, kd):
        pt_scr[i] = lax.dot_general(v_ref[0, rows_of(i), :], kd, TN_DIMS, preferred_element_type=F32)

    _run_staggered([log_decay, scaled_operands, state_increment], n_chunks)

    if zero_init:
        s0_f = jnp.zeros((dv, dk), F32)
        s0_b = jnp.zeros((dv, dk), F32)
    else:
        s0_f = st0_ref[0, 0, :, :dk]
        s0_b = st0_ref[0, 0, :, dk:]

    def fwd_step(c, s):
        if not states_only:
            sst_scr[c, :, :dk] = s.astype(BF16)
        return s * et_scr[c, :, :dk] + pt_scr[c, :, :dk]

    def bwd_step(i, s):
        c = n_chunks - 1 - i
        if not states_only:
            sst_scr[c, :, dk:] = s.astype(BF16)
        return s * et_scr[c, :, dk:] + pt_scr[c, :, dk:]

    s_f = lax.fori_loop(0, n_chunks, fwd_step, s0_f)
    s_b = lax.fori_loop(0, n_chunks, bwd_step, s0_b)

    if states_only:
        st_out_ref[0, 0, :, :dk] = s_f
        st_out_ref[0, 0, :, dk:] = s_b
        return

    def intra_scores(i, _):
        qs = qs_scr[rows_of(i), :]
        ks = ks_scr[rows_of(i), :]
        att_f = lax.dot_general(qs[:, :dk], ks[:, :dk], NT_DIMS, preferred_element_type=F32)
        att_b = lax.dot_general(qs[:, dk:], ks[:, dk:], NT_DIMS, preferred_element_type=F32)
        return (jnp.where(lower, att_f, 0.0) + jnp.where(upper, att_b, 0.0)).astype(BF16)

    def outputs(i, att):
        rows = rows_of(i)
        o = jnp.dot(att, v_ref[0, rows, :], preferred_element_type=F32)
        o = o + lax.dot_general(qe_scr[rows, :], sst_scr[i], NT_DIMS, preferred_element_type=F32)
        on = o * lax.rsqrt(jnp.mean(o * o, axis=-1, keepdims=True) + EPS)
        g = g_ref[0, rows, :].astype(F32)
        o_ref[0, rows, :] = ((on * hg_ref[...]) * _silu(g)).astype(o_ref.dtype)

    _run_staggered([intra_scores, outputs], n_chunks)


def _run_staggered(stages, n_items, newest_first=False):
    handoff = [dict() for _ in stages]
    for t in range(n_items + len(stages) - 1):
        order = range(len(stages)) if newest_first else reversed(range(len(stages)))
        for s in order:
            i = t - s
            if 0 <= i < n_items:
                arg = handoff[s - 1].pop(i) if s else None
                handoff[s][i] = stages[s](i, arg)


def _gla_scan(proj, w2, ba, head_gain, st0, *, d_model, states_only):
    b, t, _ = proj.shape
    h = GLA_HEADS
    kt, vt = d_model // 2, d_model
    dk, dv = kt // h, vt // h
    assert t % GLA_CHUNK == 0 and dk == V7X_LANES and dv == V7X_MXU_DIM and 2 * dk == V7X_MXU_DIM
    n_chunks = t // GLA_CHUNK
    a_blk = (2 * kt + 2 * vt) // V7X_MXU_DIM
    q_spec = pl.BlockSpec((1, t, dk), lambda i, j: (i, 0, j))
    k_spec = pl.BlockSpec((1, t, dk), lambda i, j: (i, 0, kt // dk + j))
    v_spec = pl.BlockSpec((1, t, dv), lambda i, j: (i, 0, 2 * kt // dv + j))
    g_spec = pl.BlockSpec((1, t, dv), lambda i, j: (i, 0, (2 * kt + vt) // dv + j))
    a_spec = pl.BlockSpec((1, t, V7X_MXU_DIM), lambda i, j: (i, 0, a_blk))
    w2_spec = pl.BlockSpec((1, V7X_MXU_DIM, 2 * dk), lambda i, j: (j, 0, 0))
    ba_spec = pl.BlockSpec((1, 1, 2 * dk), lambda i, j: (j, 0, 0))
    st_spec = pl.BlockSpec((1, 1, dv, 2 * dk), lambda i, j: (i, j, 0, 0))
    common_scratch = [pltpu.VMEM((n_chunks, dv, 2 * dk), F32), pltpu.VMEM((n_chunks, 1, 2 * dk), F32)]
    kern = functools.partial(_gla_kernel, n_chunks=n_chunks, dk=dk, dv=dv, q_scale=float(dk) ** -0.5,
                             states_only=states_only, zero_init=states_only)
    if states_only:
        return pl.pallas_call(
            kern, grid=(b, h),
            in_specs=[k_spec, v_spec, a_spec, w2_spec, ba_spec],
            out_specs=st_spec,
            out_shape=jax.ShapeDtypeStruct((b, h, dv, 2 * dk), F32),
            scratch_shapes=common_scratch,
            compiler_params=pltpu.CompilerParams(
                dimension_semantics=("parallel", "parallel"),
                vmem_limit_bytes=_vmem_limit(32 * 1024 * 1024)),
            name="gla_ctx_states",
        )(proj, proj, proj, w2, ba)
    tok = lambda width: pltpu.VMEM((t, width), BF16)
    return pl.pallas_call(
        kern, grid=(b, h),
        in_specs=[q_spec, k_spec, v_spec, g_spec, a_spec, w2_spec, ba_spec,
                  pl.BlockSpec((1, dv), lambda i, j: (0, 0)), st_spec],
        out_specs=pl.BlockSpec((1, t, dv), lambda i, j: (i, 0, j)),
        out_shape=jax.ShapeDtypeStruct((b, t, vt), BF16),
        scratch_shapes=[tok(2 * dk), tok(2 * dk), tok(2 * dk)] + common_scratch
                       + [pltpu.VMEM((n_chunks, dv, 2 * dk), BF16)],
        compiler_params=pltpu.CompilerParams(
            dimension_semantics=("parallel", "parallel"),
            vmem_limit_bytes=_vmem_limit(48 * 1024 * 1024)),
        name="gla_scan",
    )(proj, proj, proj, proj, proj, w2, ba, head_gain.reshape(1, dv), st0)


def _shift_rows(u, k):
    z = jnp.zeros((abs(k), u.shape[1]), u.dtype)
    if k > 0:
        return jnp.concatenate([z, u[:-k]], axis=0)
    return jnp.concatenate([u[-k:], z], axis=0)


def _ffn_kernel(x_ref, pre_ref, mix_gate_ref, shift_ref, scale_ref, gate_ref, gain_ref, wmix_ref,
                wup_ref, cw_ref, cb_ref, wdn_ref, fin_ref, o_ref, *, hidden, final_norm):
    rows_img, cols, d = x_ref.shape[1:]
    tm = rows_img * cols
    pre = pre_ref[0].reshape(tm, pre_ref.shape[3])
    x = x_ref[0].reshape(tm, d) + mix_gate_ref[0] * jnp.dot(pre, wmix_ref[...], preferred_element_type=F32)
    hn = _modulated_norm(x, gain_ref[...], shift_ref[0], scale_ref[0]).astype(BF16)

    def conv(u, c0):
        w = cw_ref[:, c0:c0 + HID_CHUNK]
        return (_shift_rows(u, cols) * w[0:1] + u * w[1:2] + _shift_rows(u, -cols) * w[2:3]
                + cb_ref[:, c0:c0 + HID_CHUNK])

    acc = [jnp.zeros((tm, d), F32)]

    def up_proj(j, _):
        c0 = j * HID_CHUNK
        return (jnp.dot(hn, wup_ref[:, c0:c0 + HID_CHUNK], preferred_element_type=F32),
                jnp.dot(hn, wup_ref[:, hidden + c0:hidden + c0 + HID_CHUNK], preferred_element_type=F32))

    def gate_down(j, u):
        c0 = j * HID_CHUNK
        act = (conv(u[0], c0) * _silu(conv(u[1], hidden + c0))).astype(BF16)
        acc[0] = acc[0] + jnp.dot(act, wdn_ref[c0:c0 + HID_CHUNK, :], preferred_element_type=F32)

    _run_staggered([up_proj, gate_down], hidden // HID_CHUNK, newest_first=True)
    out = x + gate_ref[0] * acc[0]
    if final_norm:
        out = out * lax.rsqrt(jnp.mean(out * out, axis=-1, keepdims=True) + EPS) * fin_ref[...]
    o_ref[0] = out.reshape(rows_img, cols, d)


def _mixer_out_conv_ffn(x, pre, mod, w_mix, gain, w_up, conv_w, conv_b, w_down, fin_gain, *, final_norm):
    b, t, d = x.shape
    hidden = w_down.shape[0]
    assert t % GRID_W == 0 and GRID_W % FFN_COLS == 0 and hidden % HID_CHUNK == 0
    rows_img = t // GRID_W
    x4 = x.reshape(b, rows_img, GRID_W, d)
    pre4 = pre.reshape(b, rows_img, GRID_W, pre.shape[2])
    mod_spec = lambda idx: pl.BlockSpec((1, 1, d), lambda i, j: (i, 0, idx))
    const = _resident
    tile = pl.BlockSpec((1, rows_img, FFN_COLS, d), lambda i, j: (i, 0, j, 0))
    out = pl.pallas_call(
        functools.partial(_ffn_kernel, hidden=hidden, final_norm=final_norm),
        grid=(b, GRID_W // FFN_COLS),
        in_specs=[tile, pl.BlockSpec((1, rows_img, FFN_COLS, pre.shape[2]), lambda i, j: (i, 0, j, 0)),
                  mod_spec(2), mod_spec(3), mod_spec(4), mod_spec(5),
                  const((1, d)), const(w_mix.shape), const(w_up.shape), const(conv_w.shape),
                  const((1, 2 * hidden)), const(w_down.shape), const((1, d))],
        out_specs=tile,
        out_shape=jax.ShapeDtypeStruct(x4.shape, F32),
        compiler_params=pltpu.CompilerParams(
            dimension_semantics=("parallel", "parallel"),
            vmem_limit_bytes=_vmem_limit(60 * 1024 * 1024)),
        name="conv_ffn",
    )(x4, pre4, mod, mod, mod, mod, gain.reshape(1, d), w_mix, w_up, conv_w, conv_b.reshape(1, 2 * hidden),
      w_down, fin_gain.reshape(1, d))
    return out.reshape(b, t, d)


def _short_conv_kernel(x_ref, shift_ref, scale_ref, gain_ref, win_ref, cw_ref, o_ref, *, width):
    x = x_ref[0]
    tm = x.shape[0]
    hn = _modulated_norm(x, gain_ref[...], shift_ref[0], scale_ref[0]).astype(BF16)
    pos = lax.broadcasted_iota(jnp.int32, (tm, SC_CHUNK), 0) % GRID_W
    has_left = pos != 0
    has_right = pos != GRID_W - 1
    for c0 in range(0, width, SC_CHUNK):
        bg, cg, v = (jnp.dot(hn, win_ref[:, base + c0:base + c0 + SC_CHUNK], preferred_element_type=F32)
                     for base in (0, width, 2 * width))
        u = cg * v
        w = cw_ref[:, c0:c0 + SC_CHUNK]
        left = jnp.where(has_left, pltpu.roll(u, 1, axis=0), 0.0)
        right = jnp.where(has_right, pltpu.roll(u, tm - 1, axis=0), 0.0)
        o_ref[0, :, c0:c0 + SC_CHUNK] = (bg * (left * w[0:1] + u * w[1:2] + right * w[2:3])).astype(o_ref.dtype)


def _short_conv(x, mod, gain, w_in, conv_w, *, tm):
    b, t, d = x.shape
    width = conv_w.shape[1]
    assert t % tm == 0 and tm % GRID_W == 0 and width % SC_CHUNK == 0
    mod_spec = lambda idx: pl.BlockSpec((1, 1, d), lambda i, j: (i, 0, idx))
    const = _resident
    tile = pl.BlockSpec((1, tm, d), lambda i, j: (i, j, 0))
    return pl.pallas_call(
        functools.partial(_short_conv_kernel, width=width),
        grid=(b, t // tm),
        in_specs=[tile, mod_spec(0), mod_spec(1), const((1, d)), const(w_in.shape), const(conv_w.shape)],
        out_specs=pl.BlockSpec((1, tm, width), lambda i, j: (i, j, 0)),
        out_shape=jax.ShapeDtypeStruct((b, t, width), BF16),
        compiler_params=pltpu.CompilerParams(
            dimension_semantics=("parallel", "parallel"),
            vmem_limit_bytes=_vmem_limit(56 * 1024 * 1024)),
        name="short_conv",
    )(x, mod, mod, gain.reshape(1, d), w_in, conv_w)


def kernel(x, c, ctx, c_ctx, ada_w, ada_b, norm_mix, norm_ffn, gla_w_in, gla_w_a2, gla_b_a,
           gla_head_norm, gla_w_out, sc_w_in, sc_conv_w, sc_w_out, ffn_w_up, ffn_conv_w,
           ffn_conv_b, ffn_w_down, final_norm):
    b, t, d = x.shape
    depth = ada_w.shape[0]
    assert depth == 2 and gla_w_in.shape[0] == 1 and sc_w_in.shape[0] == 1
    h = GLA_HEADS
    kt, vt, rank = d // 2, d, GLA_GATE_RANK
    dk = kt // h

    pad = (-(b + 1)) % 8
    cvec = jnp.concatenate([c, c_ctx[None, :], jnp.zeros((pad, d), F32)], axis=0)
    mods = _ada_mod(cvec, ada_w, ada_b)
    mod_x = [mods[i, :b].reshape(b, 1, N_MOD * d) for i in range(depth)]
    mod_ctx = mods[0, b:b + 1].reshape(1, 1, N_MOD * d)

    n_in = gla_w_in.shape[2]
    n_pad = (-n_in) % V7X_MXU_DIM
    w_in = jnp.pad(gla_w_in[0], ((0, 0), (0, n_pad))).astype(BF16)
    a2 = gla_w_a2[0].reshape(2, rank, h, dk)
    w2 = jnp.zeros((h, V7X_MXU_DIM, 2 * dk), F32)
    w2 = w2.at[:, :rank, :dk].set(a2[0].transpose(1, 0, 2))
    w2 = w2.at[:, rank:2 * rank, dk:].set(a2[1].transpose(1, 0, 2)).astype(BF16)
    ba = gla_b_a[0].reshape(2, h, dk).transpose(1, 0, 2).reshape(h, 1, 2 * dk)

    proj_ctx = _modnorm_proj(ctx, mod_ctx, norm_mix[0], w_in, shift_idx=0, scale_idx=1, tm=ctx.shape[1])
    st0 = _gla_scan(proj_ctx, w2, ba, gla_head_norm[0], None, d_model=d, states_only=True)
    proj_x = _modnorm_proj(x, mod_x[0], norm_mix[0], w_in, shift_idx=0, scale_idx=1, tm=TOKEN_TILE)
    o = _gla_scan(proj_x, w2, ba, gla_head_norm[0], st0, d_model=d, states_only=False)
    hcur = _mixer_out_conv_ffn(x, o, mod_x[0], gla_w_out[0].astype(BF16), norm_ffn[0], ffn_w_up[0].astype(BF16),
                               ffn_conv_w[0], ffn_conv_b[0], ffn_w_down[0].astype(BF16), final_norm,
                               final_norm=False)

    y = _short_conv(hcur, mod_x[1], norm_mix[1], sc_w_in[0].astype(BF16), sc_conv_w[0], tm=TOKEN_TILE)
    return _mixer_out_conv_ffn(hcur, y, mod_x[1], sc_w_out[0].astype(BF16), norm_ffn[1], ffn_w_up[1].astype(BF16),
                               ffn_conv_w[1], ffn_conv_b[1], ffn_w_down[1].astype(BF16), final_norm,
                               final_norm=True)
```

```python
import functools

import jax
import jax.numpy as jnp
from jax import lax
from jax.experimental import pallas as pl
from jax.experimental.pallas import tpu as pltpu

F32 = jnp.float32
BF16 = jnp.bfloat16

EPS = 1e-6
GRID_W = 64
N_MOD = 6
GLA_HEADS = 4
GLA_GATE_RANK = 16
GLA_GATE_TAU = 16.0
CONV_WIDTH = 3

V7X_LANES = 128
V7X_MXU_DIM = 256
V7X_VMEM_BYTES = 64 * 1024 * 1024

GLA_CHUNK = 128
GLA_STAGE_LAG = 2
TOKEN_TILE = 512
FFN_COLS = 16
HID_CHUNK = 512
SC_CHUNK = 256

NT_DIMS = (((1,), (1,)), ((), ()))
TN_DIMS = (((0,), (0,)), ((), ()))


def _vmem_limit(nbytes):
    return int(min(nbytes, V7X_VMEM_BYTES - 4 * 1024 * 1024))


def _resident(shape):
    return pl.BlockSpec(shape, lambda i, j: (0,) * len(shape), pipeline_mode=pl.Buffered(1))


def _modulated_norm(x, gain, shift, scale):
    y = x * lax.rsqrt(jnp.mean(x * x, axis=-1, keepdims=True) + EPS)
    return y * (gain * (1.0 + scale)) + shift


def _silu(x):
    return x * jax.nn.sigmoid(x)


def _ada_kernel(c_ref, w_ref, b_ref, o_ref):
    c = c_ref[...]
    sc = _silu(c).astype(BF16)
    o_ref[0] = jnp.dot(sc, w_ref[0].astype(BF16), preferred_element_type=F32) + b_ref[0]


def _ada_mod(cvec, ada_w, ada_b):
    depth, d, n = ada_w.shape
    rows = cvec.shape[0]
    tn = n // 4
    return pl.pallas_call(
        _ada_kernel,
        grid=(depth, n // tn),
        in_specs=[
            pl.BlockSpec((rows, d), lambda i, j: (0, 0)),
            pl.BlockSpec((1, d, tn), lambda i, j: (i, 0, j)),
            pl.BlockSpec((1, 1, tn), lambda i, j: (i, 0, j)),
        ],
        out_specs=pl.BlockSpec((1, rows, tn), lambda i, j: (i, 0, j)),
        out_shape=jax.ShapeDtypeStruct((depth, rows, n), F32),
        compiler_params=pltpu.CompilerParams(
            dimension_semantics=("parallel", "parallel"),
            vmem_limit_bytes=_vmem_limit(4 * d * tn * 4)),
        name="ada_mod",
    )(cvec, ada_w, ada_b.reshape(depth, 1, n))


def _gla_proj_kernel(x_ref, shift_ref, scale_ref, gain_ref, w_ref, w2_ref, ba_ref, o_ref, *, n_qkv, n_gate):
    hn = _modulated_norm(x_ref[0], gain_ref[...], shift_ref[0], scale_ref[0]).astype(BF16)
    n_chunk = 2 * V7X_MXU_DIM

    def proj(n0, width):
        return jnp.dot(hn, w_ref[:, n0:n0 + width], preferred_element_type=F32)

    n_lo = n_qkv + n_gate
    n_dec = w2_ref.shape[1]
    piece = V7X_MXU_DIM
    a_low = proj(n_lo, w2_ref.shape[0]).astype(BF16)

    def log_decay_piece(c0):
        z = jnp.dot(a_low, w2_ref[:, c0:c0 + piece], preferred_element_type=F32) + ba_ref[:, c0:c0 + piece]
        la = (jnp.minimum(z, 0.0) - jnp.log(1.0 + jnp.exp(-jnp.abs(z)))) * (1.0 / GLA_GATE_TAU)
        hi = la.astype(BF16)
        o_ref[0, :, n_lo + c0:n_lo + c0 + piece] = hi
        o_ref[0, :, n_lo + n_dec + c0:n_lo + n_dec + c0 + piece] = (la - hi.astype(F32)).astype(BF16)

    groups = list(range(n_qkv, n_qkv + n_gate, n_chunk)) + list(range(0, n_qkv, n_chunk))
    n_pieces = n_dec // piece
    assert n_pieces <= len(groups)
    for j, n0 in enumerate(groups):
        y = proj(n0, n_chunk)
        o_ref[0, :, n0:n0 + n_chunk] = (_silu(y) if n0 >= n_qkv else y).astype(o_ref.dtype)
        if j < n_pieces:
            log_decay_piece(j * piece)


def _gla_in_proj(x, mod, gain, w, w2, ba, *, kt, vt, tm):
    b, t, d = x.shape
    n_dec = w2.shape[1]
    n_qkv, n_gate = 2 * kt + vt, vt
    n_out = n_qkv + n_gate + 2 * n_dec
    per_batch = mod.shape[0] != 1
    mod_map = lambda idx: (lambda i, j: (i if per_batch else 0, 0, idx))
    assert t % tm == 0 and w.shape[1] == n_qkv + n_gate + w2.shape[0]
    return pl.pallas_call(
        functools.partial(_gla_proj_kernel, n_qkv=n_qkv, n_gate=n_gate),
        grid=(b, t // tm),
        in_specs=[
            pl.BlockSpec((1, tm, d), lambda i, j: (i, j, 0)),
            pl.BlockSpec((1, 1, d), mod_map(0)),
            pl.BlockSpec((1, 1, d), mod_map(1)),
            _resident((1, d)),
            _resident(w.shape),
            _resident(w2.shape),
            _resident(ba.shape),
        ],
        out_specs=pl.BlockSpec((1, tm, n_out), lambda i, j: (i, j, 0)),
        out_shape=jax.ShapeDtypeStruct((b, t, n_out), BF16),
        compiler_params=pltpu.CompilerParams(
            dimension_semantics=("parallel", "parallel"),
            vmem_limit_bytes=_vmem_limit(48 * 1024 * 1024)),
        name="gla_in_proj",
    )(x, mod, mod, gain.reshape(1, d), w, w2, ba)


def _gla_kernel(*refs, n_chunks, dk, dv, q_scale, states_only, zero_init):
    if states_only:
        k_ref, v_ref, hi_ref, lo_ref, st_out_ref, pt_scr, et_scr = refs
    else:
        (q_ref, k_ref, v_ref, gs_ref, hi_ref, lo_ref, hg_ref, st0_ref, o_ref,
         qs_scr, qe_scr, ks_scr, pt_scr, et_scr, sst_scr) = refs
    c_len = GLA_CHUNK
    mid = c_len // 2

    row = lax.broadcasted_iota(jnp.int32, (c_len, c_len), 0)
    col = lax.broadcasted_iota(jnp.int32, (c_len, c_len), 1)
    lower = row >= col
    upper = row <= col
    tril = jnp.where(lower, 1.0, 0.0).astype(BF16)
    triu = jnp.where(upper, 1.0, 0.0).astype(BF16)

    def rows_of(i):
        return slice(i * c_len, (i + 1) * c_len)

    def scaled_operands(i, _):
        rows = rows_of(i)
        hi, lo = hi_ref[0, rows, :], lo_ref[0, rows, :]
        pf = jnp.dot(tril, jnp.concatenate([hi[:, :dk], lo[:, :dk]], axis=1), preferred_element_type=F32)
        pb = jnp.dot(triu, jnp.concatenate([hi[:, dk:], lo[:, dk:]], axis=1), preferred_element_type=F32)
        bf = pf[:, :dk] + pf[:, dk:]
        bb = pb[:, :dk] + pb[:, dk:]
        rf, tf = bf[mid - 1:mid, :], bf[c_len - 1:c_len, :]
        rb, tb = bb[mid:mid + 1, :], bb[0:1, :]
        k = k_ref[0, rows, :].astype(F32)
        dec_f = jnp.exp(bf - rf)
        dec_b = jnp.exp(bb - rb)
        ks_f = k * (1.0 / dec_f)
        ks_b = k * (1.0 / dec_b)
        et_scr[i] = jnp.concatenate([jnp.exp(tf), jnp.exp(tb)], axis=1)
        if not states_only:
            q = q_ref[0, rows, :].astype(F32) * q_scale
            qs_f = q * dec_f
            qs_b = q * dec_b
            qs_scr[rows, :] = jnp.concatenate([qs_f, qs_b], axis=1).astype(BF16)
            qe_scr[rows, :] = jnp.concatenate([qs_f * jnp.exp(rf), qs_b * jnp.exp(rb)], axis=1).astype(BF16)
            ks_scr[rows, :] = jnp.concatenate([ks_f, ks_b], axis=1).astype(BF16)
        return jnp.concatenate([ks_f * jnp.exp(tf - rf), ks_b * jnp.exp(tb - rb)], axis=1).astype(BF16)

    def state_increment(i, kd):
        pt_scr[i] = lax.dot_general(v_ref[0, rows_of(i), :], kd, TN_DIMS, preferred_element_type=F32)

    _run_staggered([scaled_operands, state_increment], n_chunks, lag=GLA_STAGE_LAG)

    if zero_init:
        s0_f = jnp.zeros((dv, dk), F32)
        s0_b = jnp.zeros((dv, dk), F32)
    else:
        s0_f = st0_ref[0, 0, :, :dk]
        s0_b = st0_ref[0, 0, :, dk:]

    def fwd_step(c, s):
        if not states_only:
            sst_scr[c, :, :dk] = s.astype(BF16)
        return s * et_scr[c, :, :dk] + pt_scr[c, :, :dk]

    def bwd_step(i, s):
        c = n_chunks - 1 - i
        if not states_only:
            sst_scr[c, :, dk:] = s.astype(BF16)
        return s * et_scr[c, :, dk:] + pt_scr[c, :, dk:]

    s_f = lax.fori_loop(0, n_chunks, fwd_step, s0_f)
    s_b = lax.fori_loop(0, n_chunks, bwd_step, s0_b)

    if states_only:
        st_out_ref[0, 0, :, :dk] = s_f
        st_out_ref[0, 0, :, dk:] = s_b
        return

    def intra_scores(i, _):
        qs = qs_scr[rows_of(i), :]
        ks = ks_scr[rows_of(i), :]
        att_f = lax.dot_general(qs[:, :dk], ks[:, :dk], NT_DIMS, preferred_element_type=F32)
        att_b = lax.dot_general(qs[:, dk:], ks[:, dk:], NT_DIMS, preferred_element_type=F32)
        return (jnp.where(lower, att_f, 0.0) + jnp.where(upper, att_b, 0.0)).astype(BF16)

    def outputs(i, att):
        rows = rows_of(i)
        o = jnp.dot(att, v_ref[0, rows, :], preferred_element_type=F32)
        o = o + lax.dot_general(qe_scr[rows, :], sst_scr[i], NT_DIMS, preferred_element_type=F32)
        on = o * lax.rsqrt(jnp.mean(o * o, axis=-1, keepdims=True) + EPS)
        o_ref[0, rows, :] = ((on * hg_ref[...]) * gs_ref[0, rows, :].astype(F32)).astype(o_ref.dtype)

    _run_staggered([intra_scores, outputs], n_chunks, lag=GLA_STAGE_LAG)


def _run_staggered(stages, n_items, newest_first=False, lag=1):
    handoff = [dict() for _ in stages]
    for t in range(n_items + (len(stages) - 1) * lag):
        order = range(len(stages)) if newest_first else reversed(range(len(stages)))
        for s in order:
            i = t - s * lag
            if 0 <= i < n_items:
                arg = handoff[s - 1].pop(i) if s else None
                handoff[s][i] = stages[s](i, arg)


def _gla_scan(proj, head_gain, st0, *, d_model, states_only):
    b, t, _ = proj.shape
    h = GLA_HEADS
    kt, vt = d_model // 2, d_model
    dk, dv = kt // h, vt // h
    assert t % GLA_CHUNK == 0 and dk == V7X_LANES and dv == V7X_MXU_DIM and 2 * dk == V7X_MXU_DIM
    n_chunks = t // GLA_CHUNK
    hi_blk = (2 * kt + 2 * vt) // (2 * dk)
    q_spec = pl.BlockSpec((1, t, dk), lambda i, j: (i, 0, j))
    k_spec = pl.BlockSpec((1, t, dk), lambda i, j: (i, 0, kt // dk + j))
    v_spec = pl.BlockSpec((1, t, dv), lambda i, j: (i, 0, 2 * kt // dv + j))
    g_spec = pl.BlockSpec((1, t, dv), lambda i, j: (i, 0, (2 * kt + vt) // dv + j))
    hi_spec = pl.BlockSpec((1, t, 2 * dk), lambda i, j: (i, 0, hi_blk + j))
    lo_spec = pl.BlockSpec((1, t, 2 * dk), lambda i, j: (i, 0, hi_blk + h + j))
    st_spec = pl.BlockSpec((1, 1, dv, 2 * dk), lambda i, j: (i, j, 0, 0))
    common_scratch = [pltpu.VMEM((n_chunks, dv, 2 * dk), F32), pltpu.VMEM((n_chunks, 1, 2 * dk), F32)]
    kern = functools.partial(_gla_kernel, n_chunks=n_chunks, dk=dk, dv=dv, q_scale=float(dk) ** -0.5,
                             states_only=states_only, zero_init=states_only)
    if states_only:
        return pl.pallas_call(
            kern, grid=(b, h),
            in_specs=[k_spec, v_spec, hi_spec, lo_spec],
            out_specs=st_spec,
            out_shape=jax.ShapeDtypeStruct((b, h, dv, 2 * dk), F32),
            scratch_shapes=common_scratch,
            compiler_params=pltpu.CompilerParams(
                dimension_semantics=("parallel", "parallel"),
                vmem_limit_bytes=_vmem_limit(32 * 1024 * 1024)),
            name="gla_ctx_states",
        )(proj, proj, proj, proj)
    tok = lambda width: pltpu.VMEM((t, width), BF16)
    return pl.pallas_call(
        kern, grid=(b, h),
        in_specs=[q_spec, k_spec, v_spec, g_spec, hi_spec, lo_spec,
                  pl.BlockSpec((1, dv), lambda i, j: (0, 0)), st_spec],
        out_specs=pl.BlockSpec((1, t, dv), lambda i, j: (i, 0, j)),
        out_shape=jax.ShapeDtypeStruct((b, t, vt), BF16),
        scratch_shapes=[tok(2 * dk), tok(2 * dk), tok(2 * dk)] + common_scratch
                       + [pltpu.VMEM((n_chunks, dv, 2 * dk), BF16)],
        compiler_params=pltpu.CompilerParams(
            dimension_semantics=("parallel", "parallel"),
            vmem_limit_bytes=_vmem_limit(48 * 1024 * 1024)),
        name="gla_scan",
    )(proj, proj, proj, proj, proj, proj, head_gain.reshape(1, dv), st0)


def _shift_rows(u, k):
    z = jnp.zeros((abs(k), u.shape[1]), u.dtype)
    if k > 0:
        return jnp.concatenate([z, u[:-k]], axis=0)
    return jnp.concatenate([u[-k:], z], axis=0)


def _ffn_kernel(x_ref, pre_ref, mix_gate_ref, shift_ref, scale_ref, gate_ref, gain_ref, wmix_ref,
                wup_ref, cw_ref, cb_ref, wdn_ref, fin_ref, o_ref, *, hidden, final_norm):
    rows_img, cols, d = x_ref.shape[1:]
    tm = rows_img * cols
    pre = pre_ref[0].reshape(tm, pre_ref.shape[3])
    x = x_ref[0].reshape(tm, d) + mix_gate_ref[0] * jnp.dot(pre, wmix_ref[...], preferred_element_type=F32)
    hn = _modulated_norm(x, gain_ref[...], shift_ref[0], scale_ref[0]).astype(BF16)

    def conv(u, c0):
        w = cw_ref[:, c0:c0 + HID_CHUNK]
        return (_shift_rows(u, cols) * w[0:1] + u * w[1:2] + _shift_rows(u, -cols) * w[2:3]
                + cb_ref[:, c0:c0 + HID_CHUNK])

    acc = [jnp.zeros((tm, d), F32)]

    def up_proj(j, _):
        c0 = j * HID_CHUNK
        return (jnp.dot(hn, wup_ref[:, c0:c0 + HID_CHUNK], preferred_element_type=F32),
                jnp.dot(hn, wup_ref[:, hidden + c0:hidden + c0 + HID_CHUNK], preferred_element_type=F32))

    def gate_down(j, u):
        c0 = j * HID_CHUNK
        act = (conv(u[0], c0) * _silu(conv(u[1], hidden + c0))).astype(BF16)
        acc[0] = acc[0] + jnp.dot(act, wdn_ref[c0:c0 + HID_CHUNK, :], preferred_element_type=F32)

    _run_staggered([up_proj, gate_down], hidden // HID_CHUNK, newest_first=True)
    out = x + gate_ref[0] * acc[0]
    if final_norm:
        out = out * lax.rsqrt(jnp.mean(out * out, axis=-1, keepdims=True) + EPS) * fin_ref[...]
    o_ref[0] = out.reshape(rows_img, cols, d)


def _mixer_out_conv_ffn(x, pre, mod, w_mix, gain, w_up, conv_w, conv_b, w_down, fin_gain, *, final_norm):
    b, t, d = x.shape
    hidden = w_down.shape[0]
    assert t % GRID_W == 0 and GRID_W % FFN_COLS == 0 and hidden % HID_CHUNK == 0
    rows_img = t // GRID_W
    x4 = x.reshape(b, rows_img, GRID_W, d)
    pre4 = pre.reshape(b, rows_img, GRID_W, pre.shape[2])
    mod_spec = lambda idx: pl.BlockSpec((1, 1, d), lambda i, j: (i, 0, idx))
    const = _resident
    tile = pl.BlockSpec((1, rows_img, FFN_COLS, d), lambda i, j: (i, 0, j, 0))
    out = pl.pallas_call(
        functools.partial(_ffn_kernel, hidden=hidden, final_norm=final_norm),
        grid=(b, GRID_W // FFN_COLS),
        in_specs=[tile, pl.BlockSpec((1, rows_img, FFN_COLS, pre.shape[2]), lambda i, j: (i, 0, j, 0)),
                  mod_spec(2), mod_spec(3), mod_spec(4), mod_spec(5),
                  const((1, d)), const(w_mix.shape), const(w_up.shape), const(conv_w.shape),
                  const((1, 2 * hidden)), const(w_down.shape), const((1, d))],
        out_specs=tile,
        out_shape=jax.ShapeDtypeStruct(x4.shape, F32),
        compiler_params=pltpu.CompilerParams(
            dimension_semantics=("parallel", "parallel"),
            vmem_limit_bytes=_vmem_limit(60 * 1024 * 1024)),
        name="conv_ffn",
    )(x4, pre4, mod, mod, mod, mod, gain.reshape(1, d), w_mix, w_up, conv_w, conv_b.reshape(1, 2 * hidden),
      w_down, fin_gain.reshape(1, d))
    return out.reshape(b, t, d)


def _short_conv_kernel(x_ref, shift_ref, scale_ref, gain_ref, win_ref, cw_ref, o_ref, *, width):
    x = x_ref[0]
    tm = x.shape[0]
    hn = _modulated_norm(x, gain_ref[...], shift_ref[0], scale_ref[0]).astype(BF16)
    pos = lax.broadcasted_iota(jnp.int32, (tm, SC_CHUNK), 0) % GRID_W
    has_left = pos != 0
    has_right = pos != GRID_W - 1
    for c0 in range(0, width, SC_CHUNK):
        bg, cg, v = (jnp.dot(hn, win_ref[:, base + c0:base + c0 + SC_CHUNK], preferred_element_type=F32)
                     for base in (0, width, 2 * width))
        u = cg * v
        w = cw_ref[:, c0:c0 + SC_CHUNK]
        left = jnp.where(has_left, pltpu.roll(u, 1, axis=0), 0.0)
        right = jnp.where(has_right, pltpu.roll(u, tm - 1, axis=0), 0.0)
        o_ref[0, :, c0:c0 + SC_CHUNK] = (bg * (left * w[0:1] + u * w[1:2] + right * w[2:3])).astype(o_ref.dtype)


def _short_conv(x, mod, gain, w_in, conv_w, *, tm):
    b, t, d = x.shape
    width = conv_w.shape[1]
    assert t % tm == 0 and tm % GRID_W == 0 and width % SC_CHUNK == 0
    mod_spec = lambda idx: pl.BlockSpec((1, 1, d), lambda i, j: (i, 0, idx))
    const = _resident
    tile = pl.BlockSpec((1, tm, d), lambda i, j: (i, j, 0))
    return pl.pallas_call(
        functools.partial(_short_conv_kernel, width=width),
        grid=(b, t // tm),
        in_specs=[tile, mod_spec(0), mod_spec(1), const((1, d)), const(w_in.shape), const(conv_w.shape)],
        out_specs=pl.BlockSpec((1, tm, width), lambda i, j: (i, j, 0)),
        out_shape=jax.ShapeDtypeStruct((b, t, width), BF16),
        compiler_params=pltpu.CompilerParams(
            dimension_semantics=("parallel", "parallel"),
            vmem_limit_bytes=_vmem_limit(56 * 1024 * 1024)),
        name="short_conv",
    )(x, mod, mod, gain.reshape(1, d), w_in, conv_w)


def kernel(x, c, ctx, c_ctx, ada_w, ada_b, norm_mix, norm_ffn, gla_w_in, gla_w_a2, gla_b_a,
           gla_head_norm, gla_w_out, sc_w_in, sc_conv_w, sc_w_out, ffn_w_up, ffn_conv_w,
           ffn_conv_b, ffn_w_down, final_norm):
    b, t, d = x.shape
    depth = ada_w.shape[0]
    assert depth == 2 and gla_w_in.shape[0] == 1 and sc_w_in.shape[0] == 1
    h = GLA_HEADS
    kt, vt, rank = d // 2, d, GLA_GATE_RANK
    dk = kt // h

    pad = (-(b + 1)) % 8
    cvec = jnp.concatenate([c, c_ctx[None, :], jnp.zeros((pad, d), F32)], axis=0)
    mods = _ada_mod(cvec, ada_w, ada_b)
    mod_x = [mods[i, :b].reshape(b, 1, N_MOD * d) for i in range(depth)]
    mod_ctx = mods[0, b:b + 1].reshape(1, 1, N_MOD * d)

    n_in = gla_w_in.shape[2]
    n_pad = (-n_in) % V7X_MXU_DIM
    w_in = jnp.pad(gla_w_in[0], ((0, 0), (0, n_pad))).astype(BF16)
    a2 = gla_w_a2[0].reshape(2, rank, h, dk)
    w2 = jnp.zeros((h, V7X_MXU_DIM, 2 * dk), F32)
    w2 = w2.at[:, :rank, :dk].set(a2[0].transpose(1, 0, 2))
    w2 = w2.at[:, rank:2 * rank, dk:].set(a2[1].transpose(1, 0, 2)).astype(BF16)
    w2 = w2.transpose(1, 0, 2).reshape(V7X_MXU_DIM, 2 * kt)
    ba = gla_b_a[0].reshape(2, h, dk).transpose(1, 0, 2).reshape(1, 2 * kt)

    proj_ctx = _gla_in_proj(ctx, mod_ctx, norm_mix[0], w_in, w2, ba, kt=kt, vt=vt, tm=ctx.shape[1])
    st0 = _gla_scan(proj_ctx, gla_head_norm[0], None, d_model=d, states_only=True)
    proj_x = _gla_in_proj(x, mod_x[0], norm_mix[0], w_in, w2, ba, kt=kt, vt=vt, tm=TOKEN_TILE)
    o = _gla_scan(proj_x, gla_head_norm[0], st0, d_model=d, states_only=False)
    hcur = _mixer_out_conv_ffn(x, o, mod_x[0], gla_w_out[0].astype(BF16), norm_ffn[0], ffn_w_up[0].astype(BF16),
                               ffn_conv_w[0], ffn_conv_b[0], ffn_w_down[0].astype(BF16), final_norm,
                               final_norm=False)

    y = _short_conv(hcur, mod_x[1], norm_mix[1], sc_w_in[0].astype(BF16), sc_conv_w[0], tm=TOKEN_TILE)
    return _mixer_out_conv_ffn(hcur, y, mod_x[1], sc_w_out[0].astype(BF16), norm_ffn[1], ffn_w_up[1].astype(BF16),
                               ffn_conv_w[1], ffn_conv_b[1], ffn_w_down[1].astype(BF16), final_norm,
                               final_norm=True)
```

```python
import functools

import jax
import jax.numpy as jnp
from jax import lax
from jax.experimental import pallas as pl
from jax.experimental.pallas import tpu as pltpu

F32 = jnp.float32
BF16 = jnp.bfloat16

EPS = 1e-6
GRID_W = 64
N_MOD = 6
GLA_HEADS = 4
GLA_GATE_RANK = 16
GLA_GATE_TAU = 16.0
CONV_WIDTH = 3

V7X_LANES = 128
V7X_MXU_DIM = 256
V7X_VMEM_BYTES = 64 * 1024 * 1024

GLA_CHUNK = 128
GLA_STAGE_LAG = 2
TOKEN_TILE = 512
FFN_COLS = 16
HID_CHUNK = 512
SC_CHUNK = 256

NT_DIMS = (((1,), (1,)), ((), ()))
TN_DIMS = (((0,), (0,)), ((), ()))


def _vmem_limit(nbytes):
    return int(min(nbytes, V7X_VMEM_BYTES - 4 * 1024 * 1024))


def _resident(shape):
    return pl.BlockSpec(shape, lambda *_: (0,) * len(shape), pipeline_mode=pl.Buffered(1))


def _modulated_norm(x, gain, shift, scale):
    y = x * lax.rsqrt(jnp.mean(x * x, axis=-1, keepdims=True) + EPS)
    return y * (gain * (1.0 + scale)) + shift


def _silu(x):
    return x * jax.nn.sigmoid(x)


def _ada_kernel(c_ref, w_ref, b_ref, o_ref):
    c = c_ref[...]
    sc = _silu(c).astype(BF16)
    o_ref[0] = jnp.dot(sc, w_ref[0].astype(BF16), preferred_element_type=F32) + b_ref[0]


def _ada_mod(cvec, ada_w, ada_b):
    depth, d, n = ada_w.shape
    rows = cvec.shape[0]
    tn = n // 4
    return pl.pallas_call(
        _ada_kernel,
        grid=(depth, n // tn),
        in_specs=[
            pl.BlockSpec((rows, d), lambda i, j: (0, 0)),
            pl.BlockSpec((1, d, tn), lambda i, j: (i, 0, j)),
            pl.BlockSpec((1, 1, tn), lambda i, j: (i, 0, j)),
        ],
        out_specs=pl.BlockSpec((1, rows, tn), lambda i, j: (i, 0, j)),
        out_shape=jax.ShapeDtypeStruct((depth, rows, n), F32),
        compiler_params=pltpu.CompilerParams(
            dimension_semantics=("parallel", "parallel"),
            vmem_limit_bytes=_vmem_limit(4 * d * tn * 4)),
        name="ada_mod",
    )(cvec, ada_w, ada_b.reshape(depth, 1, n))


def _rolling_steps(n_tiles, head, body):
    s = pl.program_id(0)
    pl.when(s == 0)(head)
    pl.when(jnp.logical_and(s > 0, s < n_tiles))(functools.partial(body, True))
    pl.when(s == n_tiles)(functools.partial(body, False))


def _rolling_slots():
    head_slot = pl.program_id(0) % 2
    return head_slot, 1 - head_slot


def _head_tile(s, n_tiles):
    return jnp.minimum(s, n_tiles - 1)


def _body_tile(s):
    return jnp.maximum(s - 1, 0)


def _gla_proj_kernel(x_ref, shift_ref, scale_ref, gain_ref, w_ref, w2_ref, ba_ref, o_ref, *, n_qkv, n_gate):
    hn = _modulated_norm(x_ref[0], gain_ref[...], shift_ref[0], scale_ref[0]).astype(BF16)
    n_chunk = 2 * V7X_MXU_DIM

    def proj(n0, width):
        return jnp.dot(hn, w_ref[:, n0:n0 + width], preferred_element_type=F32)

    n_lo = n_qkv + n_gate
    n_dec = w2_ref.shape[1]
    piece = V7X_MXU_DIM
    a_low = proj(n_lo, w2_ref.shape[0]).astype(BF16)

    def log_decay_piece(c0):
        z = jnp.dot(a_low, w2_ref[:, c0:c0 + piece], preferred_element_type=F32) + ba_ref[:, c0:c0 + piece]
        la = (jnp.minimum(z, 0.0) - jnp.log(1.0 + jnp.exp(-jnp.abs(z)))) * (1.0 / GLA_GATE_TAU)
        hi = la.astype(BF16)
        o_ref[0, :, n_lo + c0:n_lo + c0 + piece] = hi
        o_ref[0, :, n_lo + n_dec + c0:n_lo + n_dec + c0 + piece] = (la - hi.astype(F32)).astype(BF16)

    groups = list(range(n_qkv, n_qkv + n_gate, n_chunk)) + list(range(0, n_qkv, n_chunk))
    n_pieces = n_dec // piece
    assert n_pieces <= len(groups)
    for j, n0 in enumerate(groups):
        y = proj(n0, n_chunk)
        o_ref[0, :, n0:n0 + n_chunk] = (_silu(y) if n0 >= n_qkv else y).astype(o_ref.dtype)
        if j < n_pieces:
            log_decay_piece(j * piece)


def _gla_in_proj(x, mod, gain, w, w2, ba, *, kt, vt, tm):
    b, t, d = x.shape
    n_dec = w2.shape[1]
    n_qkv, n_gate = 2 * kt + vt, vt
    n_out = n_qkv + n_gate + 2 * n_dec
    per_batch = mod.shape[0] != 1
    mod_map = lambda idx: (lambda i, j: (i if per_batch else 0, 0, idx))
    assert t % tm == 0 and w.shape[1] == n_qkv + n_gate + w2.shape[0]
    return pl.pallas_call(
        functools.partial(_gla_proj_kernel, n_qkv=n_qkv, n_gate=n_gate),
        grid=(b, t // tm),
        in_specs=[
            pl.BlockSpec((1, tm, d), lambda i, j: (i, j, 0)),
            pl.BlockSpec((1, 1, d), mod_map(0)),
            pl.BlockSpec((1, 1, d), mod_map(1)),
            _resident((1, d)),
            _resident(w.shape),
            _resident(w2.shape),
            _resident(ba.shape),
        ],
        out_specs=pl.BlockSpec((1, tm, n_out), lambda i, j: (i, j, 0)),
        out_shape=jax.ShapeDtypeStruct((b, t, n_out), BF16),
        compiler_params=pltpu.CompilerParams(
            dimension_semantics=("parallel", "parallel"),
            vmem_limit_bytes=_vmem_limit(48 * 1024 * 1024)),
        name="gla_in_proj",
    )(x, mod, mod, gain.reshape(1, d), w, w2, ba)


def _gla_kernel(*refs, n_chunks, dk, dv, q_scale, states_only, zero_init):
    if states_only:
        k_ref, v_ref, hi_ref, lo_ref, st_out_ref, pt_scr, et_scr = refs
    else:
        (q_ref, k_ref, v_ref, gs_ref, hi_ref, lo_ref, hg_ref, st0_ref, o_ref,
         qs_scr, qe_scr, kst_scr, pt_scr, et_scr, sst_scr) = refs
    c_len = GLA_CHUNK
    mid = c_len // 2

    row = lax.broadcasted_iota(jnp.int32, (c_len, c_len), 0)
    col = lax.broadcasted_iota(jnp.int32, (c_len, c_len), 1)
    lower = row >= col
    upper = row <= col
    tril = jnp.where(lower, 1.0, 0.0).astype(BF16)
    triu = jnp.where(upper, 1.0, 0.0).astype(BF16)

    def rows_of(i):
        return slice(i * c_len, (i + 1) * c_len)

    def scaled_operands(i, _):
        rows = rows_of(i)
        hi, lo = hi_ref[0, rows, :], lo_ref[0, rows, :]
        pf = jnp.dot(tril, jnp.concatenate([hi[:, :dk], lo[:, :dk]], axis=1), preferred_element_type=F32)
        pb = jnp.dot(triu, jnp.concatenate([hi[:, dk:], lo[:, dk:]], axis=1), preferred_element_type=F32)
        bf = pf[:, :dk] + pf[:, dk:]
        bb = pb[:, :dk] + pb[:, dk:]
        rf, tf = bf[mid - 1:mid, :], bf[c_len - 1:c_len, :]
        rb, tb = bb[mid:mid + 1, :], bb[0:1, :]
        k = k_ref[0, rows, :].astype(F32)
        dec_f = jnp.exp(bf - rf)
        dec_b = jnp.exp(bb - rb)
        ks_f = k * (1.0 / dec_f)
        ks_b = k * (1.0 / dec_b)
        et_scr[i] = jnp.concatenate([jnp.exp(tf), jnp.exp(tb)], axis=1)
        if not states_only:
            q = q_ref[0, rows, :].astype(F32) * q_scale
            qs_f = q * dec_f
            qs_b = q * dec_b
            qs_scr[rows, :] = jnp.concatenate([qs_f, qs_b], axis=1).astype(BF16)
            qe_scr[rows, :] = jnp.concatenate([qs_f * jnp.exp(rf), qs_b * jnp.exp(rb)], axis=1).astype(BF16)
            kst_scr[:dk, rows] = ks_f.T.astype(BF16)
            kst_scr[dk:, rows] = ks_b.T.astype(BF16)
        return jnp.concatenate([ks_f * jnp.exp(tf - rf), ks_b * jnp.exp(tb - rb)], axis=1).astype(BF16)

    def state_increment(i, kd):
        pt_scr[i] = lax.dot_general(v_ref[0, rows_of(i), :], kd, TN_DIMS, preferred_element_type=F32)

    _run_staggered([scaled_operands, state_increment], n_chunks, lag=GLA_STAGE_LAG)

    if zero_init:
        s0_f = jnp.zeros((dv, dk), F32)
        s0_b = jnp.zeros((dv, dk), F32)
    else:
        s0_f = st0_ref[0, 0, :, :dk]
        s0_b = st0_ref[0, 0, :, dk:]

    s_f, s_b = s0_f, s0_b
    for c in range(n_chunks):
        if not states_only:
            sst_scr[c, :dk, :] = s_f.T.astype(BF16)
        s_f = s_f * et_scr[c, :, :dk] + pt_scr[c, :, :dk]
    for c in reversed(range(n_chunks)):
        if not states_only:
            sst_scr[c, dk:, :] = s_b.T.astype(BF16)
        s_b = s_b * et_scr[c, :, dk:] + pt_scr[c, :, dk:]

    if states_only:
        st_out_ref[0, 0, :, :dk] = s_f
        st_out_ref[0, 0, :, dk:] = s_b
        return

    def intra_scores(i, _):
        qs = qs_scr[rows_of(i), :]
        att_f = jnp.dot(qs[:, :dk], kst_scr[:dk, rows_of(i)], preferred_element_type=F32)
        att_b = jnp.dot(qs[:, dk:], kst_scr[dk:, rows_of(i)], preferred_element_type=F32)
        return (jnp.where(lower, att_f, 0.0) + jnp.where(upper, att_b, 0.0)).astype(BF16)

    def outputs(i, att):
        rows = rows_of(i)
        o = jnp.dot(att, v_ref[0, rows, :], preferred_element_type=F32)
        o = o + jnp.dot(qe_scr[rows, :], sst_scr[i], preferred_element_type=F32)
        return o, jnp.mean(o * o, axis=-1, keepdims=True)

    def normalise(i, o_ms):
        rows = rows_of(i)
        on = o_ms[0] * lax.rsqrt(o_ms[1] + EPS)
        o_ref[0, rows, :] = ((on * hg_ref[...]) * gs_ref[0, rows, :].astype(F32)).astype(o_ref.dtype)

    _run_staggered([intra_scores, outputs, normalise], n_chunks, lag=GLA_STAGE_LAG)


def _run_staggered(stages, n_items, newest_first=False, lag=1):
    handoff = [dict() for _ in stages]
    for t in range(n_items + (len(stages) - 1) * lag):
        order = range(len(stages)) if newest_first else reversed(range(len(stages)))
        for s in order:
            i = t - s * lag
            if 0 <= i < n_items:
                arg = handoff[s - 1].pop(i) if s else None
                handoff[s][i] = stages[s](i, arg)


def _gla_scan(proj, head_gain, st0, *, d_model, states_only):
    b, t, _ = proj.shape
    h = GLA_HEADS
    kt, vt = d_model // 2, d_model
    dk, dv = kt // h, vt // h
    assert t % GLA_CHUNK == 0 and dk == V7X_LANES and dv == V7X_MXU_DIM and 2 * dk == V7X_MXU_DIM
    n_chunks = t // GLA_CHUNK
    hi_blk = (2 * kt + 2 * vt) // (2 * dk)
    q_spec = pl.BlockSpec((1, t, dk), lambda i, j: (i, 0, j))
    k_spec = pl.BlockSpec((1, t, dk), lambda i, j: (i, 0, kt // dk + j))
    v_spec = pl.BlockSpec((1, t, dv), lambda i, j: (i, 0, 2 * kt // dv + j))
    g_spec = pl.BlockSpec((1, t, dv), lambda i, j: (i, 0, (2 * kt + vt) // dv + j))
    hi_spec = pl.BlockSpec((1, t, 2 * dk), lambda i, j: (i, 0, hi_blk + j))
    lo_spec = pl.BlockSpec((1, t, 2 * dk), lambda i, j: (i, 0, hi_blk + h + j))
    st_spec = pl.BlockSpec((1, 1, dv, 2 * dk), lambda i, j: (i, j, 0, 0))
    common_scratch = [pltpu.VMEM((n_chunks, dv, 2 * dk), F32), pltpu.VMEM((n_chunks, 1, 2 * dk), F32)]
    kern = functools.partial(_gla_kernel, n_chunks=n_chunks, dk=dk, dv=dv, q_scale=float(dk) ** -0.5,
                             states_only=states_only, zero_init=states_only)
    if states_only:
        return pl.pallas_call(
            kern, grid=(b, h),
            in_specs=[k_spec, v_spec, hi_spec, lo_spec],
            out_specs=st_spec,
            out_shape=jax.ShapeDtypeStruct((b, h, dv, 2 * dk), F32),
            scratch_shapes=common_scratch,
            compiler_params=pltpu.CompilerParams(
                dimension_semantics=("parallel", "parallel"),
                vmem_limit_bytes=_vmem_limit(32 * 1024 * 1024)),
            name="gla_ctx_states",
        )(proj, proj, proj, proj)
    tok = lambda width: pltpu.VMEM((t, width), BF16)
    return pl.pallas_call(
        kern, grid=(b, h),
        in_specs=[q_spec, k_spec, v_spec, g_spec, hi_spec, lo_spec,
                  pl.BlockSpec((1, dv), lambda i, j: (0, 0)), st_spec],
        out_specs=pl.BlockSpec((1, t, dv), lambda i, j: (i, 0, j)),
        out_shape=jax.ShapeDtypeStruct((b, t, vt), BF16),
        scratch_shapes=[tok(2 * dk), tok(2 * dk), pltpu.VMEM((2 * dk, t), BF16)] + common_scratch
                       + [pltpu.VMEM((n_chunks, 2 * dk, dv), BF16)],
        compiler_params=pltpu.CompilerParams(
            dimension_semantics=("parallel", "parallel"),
            vmem_limit_bytes=_vmem_limit(48 * 1024 * 1024)),
        name="gla_scan",
    )(proj, proj, proj, proj, proj, proj, head_gain.reshape(1, dv), st0)


def _shift_rows(u, k):
    z = jnp.zeros((abs(k), u.shape[1]), u.dtype)
    if k > 0:
        return jnp.concatenate([z, u[:-k]], axis=0)
    return jnp.concatenate([u[-k:], z], axis=0)


def _ffn_kernel(x_ref, pre_ref, mix_gate_ref, shift_ref, scale_ref, gate_ref, gain_ref, wmix_ref,
                wup_ref, cw_ref, cb_ref, wdn_ref, fin_ref, o_ref, x_scr, hn_scr, *, hidden, final_norm,
                n_tiles):
    rows_img, cols, d = x_ref.shape[1:]
    tm = rows_img * cols
    head_slot, body_slot = _rolling_slots()

    def head_matmul():
        return jnp.dot(pre_ref[0].reshape(tm, pre_ref.shape[3]), wmix_ref[...], preferred_element_type=F32)

    def head_finish(y):
        x = x_ref[0].reshape(tm, d) + mix_gate_ref[0] * y
        x_scr[head_slot] = x
        hn_scr[head_slot] = _modulated_norm(x, gain_ref[...], shift_ref[0], scale_ref[0]).astype(BF16)

    def conv(u, c0):
        w = cw_ref[:, c0:c0 + HID_CHUNK]
        return (_shift_rows(u, cols) * w[0:1] + u * w[1:2] + _shift_rows(u, -cols) * w[2:3]
                + cb_ref[:, c0:c0 + HID_CHUNK])

    def body(with_head):
        acc = [jnp.zeros((tm, d), F32)]
        head_y = []

        def up_proj(j, _):
            c0 = j * HID_CHUNK
            hn = hn_scr[body_slot]
            u = (jnp.dot(hn, wup_ref[:, c0:c0 + HID_CHUNK], preferred_element_type=F32),
                 jnp.dot(hn, wup_ref[:, hidden + c0:hidden + c0 + HID_CHUNK], preferred_element_type=F32))
            if with_head and j == 1:
                head_y.append(head_matmul())
            return u

        def gate_down(j, u):
            c0 = j * HID_CHUNK
            act = (conv(u[0], c0) * _silu(conv(u[1], hidden + c0))).astype(BF16)
            acc[0] = acc[0] + jnp.dot(act, wdn_ref[c0:c0 + HID_CHUNK, :], preferred_element_type=F32)
            if with_head and j == 1:
                head_finish(head_y.pop())

        _run_staggered([up_proj, gate_down], hidden // HID_CHUNK, newest_first=True)
        out = x_scr[body_slot] + gate_ref[0] * acc[0]
        if final_norm:
            out = out * lax.rsqrt(jnp.mean(out * out, axis=-1, keepdims=True) + EPS) * fin_ref[...]
        o_ref[0] = out.reshape(rows_img, cols, d)

    _rolling_steps(n_tiles, lambda: head_finish(head_matmul()), body)


def _mixer_out_conv_ffn(x, pre, mod, w_mix, gain, w_up, conv_w, conv_b, w_down, fin_gain, *, final_norm):
    b, t, d = x.shape
    hidden = w_down.shape[0]
    assert t % GRID_W == 0 and GRID_W % FFN_COLS == 0 and hidden % HID_CHUNK == 0 and hidden >= 2 * HID_CHUNK
    rows_img = t // GRID_W
    col_tiles = GRID_W // FFN_COLS
    n_tiles = b * col_tiles
    tm = rows_img * FFN_COLS
    x4 = x.reshape(b, rows_img, GRID_W, d)
    pre4 = pre.reshape(b, rows_img, GRID_W, pre.shape[2])
    head = lambda s: _head_tile(s, n_tiles)
    back = _body_tile
    head_tile = lambda width: pl.BlockSpec((1, rows_img, FFN_COLS, width),
                                           lambda s: (head(s) // col_tiles, 0, head(s) % col_tiles, 0))
    head_mod = lambda idx: pl.BlockSpec((1, 1, d), lambda s: (head(s) // col_tiles, 0, idx))
    const = _resident
    out = pl.pallas_call(
        functools.partial(_ffn_kernel, hidden=hidden, final_norm=final_norm, n_tiles=n_tiles),
        grid=(n_tiles + 1,),
        in_specs=[head_tile(d), head_tile(pre.shape[2]), head_mod(2), head_mod(3), head_mod(4),
                  pl.BlockSpec((1, 1, d), lambda s: (back(s) // col_tiles, 0, 5)),
                  const((1, d)), const(w_mix.shape), const(w_up.shape), const(conv_w.shape),
                  const((1, 2 * hidden)), const(w_down.shape), const((1, d))],
        out_specs=pl.BlockSpec((1, rows_img, FFN_COLS, d),
                               lambda s: (back(s) // col_tiles, 0, back(s) % col_tiles, 0)),
        out_shape=jax.ShapeDtypeStruct(x4.shape, F32),
        scratch_shapes=[pltpu.VMEM((2, tm, d), F32), pltpu.VMEM((2, tm, d), BF16)],
        compiler_params=pltpu.CompilerParams(
            dimension_semantics=("arbitrary",),
            vmem_limit_bytes=_vmem_limit(60 * 1024 * 1024)),
        name="conv_ffn",
    )(x4, pre4, mod, mod, mod, mod, gain.reshape(1, d), w_mix, w_up, conv_w, conv_b.reshape(1, 2 * hidden),
      w_down, fin_gain.reshape(1, d))
    return out.reshape(b, t, d)


def _short_conv_kernel(x_ref, shift_ref, scale_ref, gain_ref, win_ref, cw_ref, o_ref, *, width):
    x = x_ref[0]
    tm = x.shape[0]
    hn = _modulated_norm(x, gain_ref[...], shift_ref[0], scale_ref[0]).astype(BF16)
    pos = lax.broadcasted_iota(jnp.int32, (tm, SC_CHUNK), 0) % GRID_W
    has_left = pos != 0
    has_right = pos != GRID_W - 1
    for c0 in range(0, width, SC_CHUNK):
        cg, v, bg = (jnp.dot(hn, win_ref[:, base + c0:base + c0 + SC_CHUNK], preferred_element_type=F32)
                     for base in (width, 2 * width, 0))
        u = cg * v
        w = cw_ref[:, c0:c0 + SC_CHUNK]
        left = jnp.where(has_left, pltpu.roll(u, 1, axis=0), 0.0)
        right = jnp.where(has_right, pltpu.roll(u, tm - 1, axis=0), 0.0)
        o_ref[0, :, c0:c0 + SC_CHUNK] = (bg * (left * w[0:1] + u * w[1:2] + right * w[2:3])).astype(o_ref.dtype)


def _short_conv(x, mod, gain, w_in, conv_w, *, tm):
    b, t, d = x.shape
    width = conv_w.shape[1]
    assert t % tm == 0 and tm % GRID_W == 0 and width % SC_CHUNK == 0
    mod_spec = lambda idx: pl.BlockSpec((1, 1, d), lambda i, j: (i, 0, idx))
    const = _resident
    tile = pl.BlockSpec((1, tm, d), lambda i, j: (i, j, 0))
    return pl.pallas_call(
        functools.partial(_short_conv_kernel, width=width),
        grid=(b, t // tm),
        in_specs=[tile, mod_spec(0), mod_spec(1), const((1, d)), const(w_in.shape), const(conv_w.shape)],
        out_specs=pl.BlockSpec((1, tm, width), lambda i, j: (i, j, 0)),
        out_shape=jax.ShapeDtypeStruct((b, t, width), BF16),
        compiler_params=pltpu.CompilerParams(
            dimension_semantics=("parallel", "parallel"),
            vmem_limit_bytes=_vmem_limit(56 * 1024 * 1024)),
        name="short_conv",
    )(x, mod, mod, gain.reshape(1, d), w_in, conv_w)


def kernel(x, c, ctx, c_ctx, ada_w, ada_b, norm_mix, norm_ffn, gla_w_in, gla_w_a2, gla_b_a,
           gla_head_norm, gla_w_out, sc_w_in, sc_conv_w, sc_w_out, ffn_w_up, ffn_conv_w,
           ffn_conv_b, ffn_w_down, final_norm):
    b, t, d = x.shape
    depth = ada_w.shape[0]
    assert depth == 2 and gla_w_in.shape[0] == 1 and sc_w_in.shape[0] == 1
    h = GLA_HEADS
    kt, vt, rank = d // 2, d, GLA_GATE_RANK
    dk = kt // h

    pad = (-(b + 1)) % 8
    cvec = jnp.concatenate([c, c_ctx[None, :], jnp.zeros((pad, d), F32)], axis=0)
    mods = _ada_mod(cvec, ada_w, ada_b)
    mod_x = [mods[i, :b].reshape(b, 1, N_MOD * d) for i in range(depth)]
    mod_ctx = mods[0, b:b + 1].reshape(1, 1, N_MOD * d)

    n_in = gla_w_in.shape[2]
    n_pad = (-n_in) % V7X_MXU_DIM
    w_in = jnp.pad(gla_w_in[0], ((0, 0), (0, n_pad))).astype(BF16)
    a2 = gla_w_a2[0].reshape(2, rank, h, dk)
    w2 = jnp.zeros((h, V7X_MXU_DIM, 2 * dk), F32)
    w2 = w2.at[:, :rank, :dk].set(a2[0].transpose(1, 0, 2))
    w2 = w2.at[:, rank:2 * rank, dk:].set(a2[1].transpose(1, 0, 2)).astype(BF16)
    w2 = w2.transpose(1, 0, 2).reshape(V7X_MXU_DIM, 2 * kt)
    ba = gla_b_a[0].reshape(2, h, dk).transpose(1, 0, 2).reshape(1, 2 * kt)

    proj_ctx = _gla_in_proj(ctx, mod_ctx, norm_mix[0], w_in, w2, ba, kt=kt, vt=vt, tm=ctx.shape[1])
    st0 = _gla_scan(proj_ctx, gla_head_norm[0], None, d_model=d, states_only=True)
    proj_x = _gla_in_proj(x, mod_x[0], norm_mix[0], w_in, w2, ba, kt=kt, vt=vt, tm=TOKEN_TILE)
    o = _gla_scan(proj_x, gla_head_norm[0], st0, d_model=d, states_only=False)
    hcur = _mixer_out_conv_ffn(x, o, mod_x[0], gla_w_out[0].astype(BF16), norm_ffn[0], ffn_w_up[0].astype(BF16),
                               ffn_conv_w[0], ffn_conv_b[0], ffn_w_down[0].astype(BF16), final_norm,
                               final_norm=False)

    y = _short_conv(hcur, mod_x[1], norm_mix[1], sc_w_in[0].astype(BF16), sc_conv_w[0], tm=TOKEN_TILE)
    return _mixer_out_conv_ffn(hcur, y, mod_x[1], sc_w_out[0].astype(BF16), norm_ffn[1], ffn_w_up[1].astype(BF16),
                               ffn_conv_w[1], ffn_conv_b[1], ffn_w_down[1].astype(BF16), final_norm,
                               final_norm=True)
```

```python
import functools

import jax
import jax.numpy as jnp
from jax import lax
from jax.experimental import pallas as pl
from jax.experimental.pallas import tpu as pltpu

F32 = jnp.float32
BF16 = jnp.bfloat16

EPS = 1e-6
GRID_W = 64
N_MOD = 6
GLA_HEADS = 4
GLA_GATE_RANK = 16
GLA_GATE_TAU = 16.0
CONV_WIDTH = 3

V7X_LANES = 128
V7X_MXU_DIM = 256
V7X_VMEM_BYTES = 64 * 1024 * 1024

GLA_CHUNK = 128
GLA_STAGE_LAG = 2
TOKEN_TILE = 512
FFN_COLS = 16
HID_CHUNK = 512
SC_CHUNK = 256

TN_DIMS = (((0,), (0,)), ((), ()))


def _vmem_limit(nbytes):
    return int(min(nbytes, V7X_VMEM_BYTES - 4 * 1024 * 1024))


def _resident(shape):
    return pl.BlockSpec(shape, lambda *_: (0,) * len(shape), pipeline_mode=pl.Buffered(1))


def _resident_layer(stacked_shape, layer):
    return pl.BlockSpec((1,) + tuple(stacked_shape[1:]), lambda *_: (layer, 0, 0), pipeline_mode=pl.Buffered(1))


def _modulated_norm(x, gain, shift, scale):
    y = x * lax.rsqrt(jnp.mean(x * x, axis=-1, keepdims=True) + EPS)
    return y * (gain * (1.0 + scale)) + shift


def _silu(x):
    return x * jax.nn.sigmoid(x)


def _run_staggered(stages, n_items, newest_first=False, lag=1):
    handoff = [dict() for _ in stages]
    for t in range(n_items + (len(stages) - 1) * lag):
        order = range(len(stages)) if newest_first else reversed(range(len(stages)))
        for s in order:
            i = t - s * lag
            if 0 <= i < n_items:
                arg = handoff[s - 1].pop(i) if s else None
                handoff[s][i] = stages[s](i, arg)


def _ada_kernel(c_ref, w_ref, b_ref, o_ref):
    c = c_ref[...]
    sc = _silu(c).astype(BF16)
    o_ref[0] = jnp.dot(sc, w_ref[0].astype(BF16), preferred_element_type=F32) + b_ref[0]


def _ada_mod(cvec, ada_w, ada_b):
    depth, d, n = ada_w.shape
    rows = cvec.shape[0]
    tn = n // 4
    return pl.pallas_call(
        _ada_kernel,
        grid=(depth, n // tn),
        in_specs=[
            pl.BlockSpec((rows, d), lambda i, j: (0, 0)),
            pl.BlockSpec((1, d, tn), lambda i, j: (i, 0, j)),
            pl.BlockSpec((1, 1, tn), lambda i, j: (i, 0, j)),
        ],
        out_specs=pl.BlockSpec((1, rows, tn), lambda i, j: (i, 0, j)),
        out_shape=jax.ShapeDtypeStruct((depth, rows, n), F32),
        compiler_params=pltpu.CompilerParams(
            dimension_semantics=("parallel", "parallel"),
            vmem_limit_bytes=_vmem_limit(4 * d * tn * 4)),
        name="ada_mod",
    )(cvec, ada_w, ada_b.reshape(depth, 1, n))


def _gla_proj_kernel(x_ref, shift_ref, scale_ref, gain_ref, w_ref, w2_ref, ba_ref, o_ref, *, groups, w_low,
                     o_dec):
    hn = _modulated_norm(x_ref[0], gain_ref[...], shift_ref[0], scale_ref[0]).astype(BF16)
    n_chunk = 2 * V7X_MXU_DIM

    def proj(n0, width):
        return jnp.dot(hn, w_ref[:, n0:n0 + width], preferred_element_type=F32)

    n_dec = w2_ref.shape[1]
    piece = V7X_MXU_DIM
    a_low = proj(w_low, w2_ref.shape[0]).astype(BF16)

    def log_decay_piece(c0):
        z = jnp.dot(a_low, w2_ref[:, c0:c0 + piece], preferred_element_type=F32) + ba_ref[:, c0:c0 + piece]
        la = (jnp.minimum(z, 0.0) - jnp.log(1.0 + jnp.exp(-jnp.abs(z)))) * (1.0 / GLA_GATE_TAU)
        hi = la.astype(BF16)
        o_ref[0, :, o_dec + c0:o_dec + c0 + piece] = hi
        o_ref[0, :, o_dec + n_dec + c0:o_dec + n_dec + c0 + piece] = (la - hi.astype(F32)).astype(BF16)

    pieces = list(range(0, n_dec, piece))
    for w0, o0, gated in groups:
        y = proj(w0, n_chunk)
        o_ref[0, :, o0:o0 + n_chunk] = (_silu(y) if gated else y).astype(o_ref.dtype)
        if pieces:
            log_decay_piece(pieces.pop(0))
    for c0 in pieces:
        log_decay_piece(c0)


def _gla_in_proj(x, mod, gain, w, w2, ba, *, kt, vt, tm, with_query):
    b, t, d = x.shape
    n_dec = w2.shape[1]
    n_chunk = 2 * V7X_MXU_DIM
    w_low = 2 * kt + 2 * vt
    if with_query:
        groups = ([(c, c, True) for c in range(2 * kt + vt, w_low, n_chunk)]
                  + [(c, c, False) for c in range(0, 2 * kt + vt, n_chunk)])
        o_dec = w_low
    else:
        groups = [(kt + c, c, False) for c in range(0, kt + vt, n_chunk)]
        o_dec = kt + vt
    n_out = o_dec + 2 * n_dec
    per_batch = mod.shape[0] != 1
    mod_map = lambda idx: (lambda i, j: (i if per_batch else 0, 0, idx))
    assert t % tm == 0 and w.shape[1] == w_low + w2.shape[0]
    return pl.pallas_call(
        functools.partial(_gla_proj_kernel, groups=tuple(groups), w_low=w_low, o_dec=o_dec),
        grid=(b, t // tm),
        in_specs=[
            pl.BlockSpec((1, tm, d), lambda i, j: (i, j, 0)),
            pl.BlockSpec((1, 1, d), mod_map(0)),
            pl.BlockSpec((1, 1, d), mod_map(1)),
            _resident((1, d)),
            _resident(w.shape),
            _resident(w2.shape),
            _resident(ba.shape),
        ],
        out_specs=pl.BlockSpec((1, tm, n_out), lambda i, j: (i, j, 0)),
        out_shape=jax.ShapeDtypeStruct((b, t, n_out), BF16),
        compiler_params=pltpu.CompilerParams(
            dimension_semantics=("parallel", "parallel"),
            vmem_limit_bytes=_vmem_limit(48 * 1024 * 1024)),
        name="gla_in_proj",
    )(x, mod, mod, gain.reshape(1, d), w, w2, ba)


def _gla_kernel(q_ref, k_ref, v_ref, gs_ref, hi_ref, lo_ref, kc_ref, vc_ref, hic_ref, loc_ref, hg_ref, o_ref,
                qs_scr, qe_scr, kst_scr, pt_scr, et_scr, ptc_scr, etc_scr, sst_scr, *,
                n_chunks, n_ctx_chunks, dk, dv, q_scale):
    c_len = GLA_CHUNK
    mid = c_len // 2

    row = lax.broadcasted_iota(jnp.int32, (c_len, c_len), 0)
    col = lax.broadcasted_iota(jnp.int32, (c_len, c_len), 1)
    lower = row >= col
    upper = row <= col
    tril = jnp.where(lower, 1.0, 0.0).astype(BF16)
    triu = jnp.where(upper, 1.0, 0.0).astype(BF16)

    def rows_of(i):
        return slice(i * c_len, (i + 1) * c_len)

    def prepare_stages(k_ref, v_ref, hi_ref, lo_ref, et_dst, pt_dst, with_query):
        def scaled_operands(i, _):
            rows = rows_of(i)
            hi, lo = hi_ref[0, rows, :], lo_ref[0, rows, :]
            pf = jnp.dot(tril, jnp.concatenate([hi[:, :dk], lo[:, :dk]], axis=1), preferred_element_type=F32)
            pb = jnp.dot(triu, jnp.concatenate([hi[:, dk:], lo[:, dk:]], axis=1), preferred_element_type=F32)
            bf = pf[:, :dk] + pf[:, dk:]
            bb = pb[:, :dk] + pb[:, dk:]
            rf, tf = bf[mid - 1:mid, :], bf[c_len - 1:c_len, :]
            rb, tb = bb[mid:mid + 1, :], bb[0:1, :]
            k = k_ref[0, rows, :].astype(F32)
            dec_f = jnp.exp(bf - rf)
            dec_b = jnp.exp(bb - rb)
            ks_f = k * (1.0 / dec_f)
            ks_b = k * (1.0 / dec_b)
            et_dst[i] = jnp.concatenate([jnp.exp(tf), jnp.exp(tb)], axis=1)
            if with_query:
                q = q_ref[0, rows, :].astype(F32) * q_scale
                qs_f = q * dec_f
                qs_b = q * dec_b
                qs_scr[rows, :] = jnp.concatenate([qs_f, qs_b], axis=1).astype(BF16)
                qe_scr[rows, :] = jnp.concatenate([qs_f * jnp.exp(rf), qs_b * jnp.exp(rb)], axis=1).astype(BF16)
                kst_scr[:dk, rows] = ks_f.T.astype(BF16)
                kst_scr[dk:, rows] = ks_b.T.astype(BF16)
            return jnp.concatenate([ks_f * jnp.exp(tf - rf), ks_b * jnp.exp(tb - rb)], axis=1).astype(BF16)

        def state_increment(i, kd):
            pt_dst[i] = lax.dot_general(v_ref[0, rows_of(i), :], kd, TN_DIMS, preferred_element_type=F32)

        return [scaled_operands, state_increment]

    ctx_stages = prepare_stages(kc_ref, vc_ref, hic_ref, loc_ref, etc_scr, ptc_scr, False)
    lat_stages = prepare_stages(k_ref, v_ref, hi_ref, lo_ref, et_scr, pt_scr, True)
    both = [lambda i, arg, c=c, l=l: c(i, arg) if i < n_ctx_chunks else l(i - n_ctx_chunks, arg)
            for c, l in zip(ctx_stages, lat_stages)]
    _run_staggered(both, n_ctx_chunks + n_chunks, lag=GLA_STAGE_LAG)

    s_f = jnp.zeros((dv, dk), F32)
    s_b = jnp.zeros((dv, dk), F32)
    for c in range(n_ctx_chunks):
        s_f = s_f * etc_scr[c, :, :dk] + ptc_scr[c, :, :dk]
    for c in reversed(range(n_ctx_chunks)):
        s_b = s_b * etc_scr[c, :, dk:] + ptc_scr[c, :, dk:]
    for c in range(n_chunks):
        sst_scr[c, :dk, :] = s_f.T.astype(BF16)
        s_f = s_f * et_scr[c, :, :dk] + pt_scr[c, :, :dk]
    for c in reversed(range(n_chunks)):
        sst_scr[c, dk:, :] = s_b.T.astype(BF16)
        s_b = s_b * et_scr[c, :, dk:] + pt_scr[c, :, dk:]

    def intra_scores(i, _):
        qs = qs_scr[rows_of(i), :]
        att_f = jnp.dot(qs[:, :dk], kst_scr[:dk, rows_of(i)], preferred_element_type=F32)
        att_b = jnp.dot(qs[:, dk:], kst_scr[dk:, rows_of(i)], preferred_element_type=F32)
        return (jnp.where(lower, att_f, 0.0) + jnp.where(upper, att_b, 0.0)).astype(BF16)

    def outputs(i, att):
        rows = rows_of(i)
        o = jnp.dot(att, v_ref[0, rows, :], preferred_element_type=F32)
        o = o + jnp.dot(qe_scr[rows, :], sst_scr[i], preferred_element_type=F32)
        return o, jnp.mean(o * o, axis=-1, keepdims=True)

    def normalise(i, o_ms):
        rows = rows_of(i)
        on = o_ms[0] * lax.rsqrt(o_ms[1] + EPS)
        o_ref[0, rows, :] = ((on * hg_ref[...]) * gs_ref[0, rows, :].astype(F32)).astype(o_ref.dtype)

    _run_staggered([intra_scores, outputs, normalise], n_chunks, lag=GLA_STAGE_LAG)


def _gla_scan(proj, proj_ctx, head_gain, *, d_model):
    b, t, _ = proj.shape
    tc = proj_ctx.shape[1]
    h = GLA_HEADS
    kt, vt = d_model // 2, d_model
    dk, dv = kt // h, vt // h
    assert t % GLA_CHUNK == 0 and tc % GLA_CHUNK == 0
    assert dk == V7X_LANES and dv == V7X_MXU_DIM and 2 * dk == V7X_MXU_DIM
    n_chunks, n_ctx_chunks = t // GLA_CHUNK, tc // GLA_CHUNK

    def col_spec(rows, width, col0):
        return pl.BlockSpec((1, rows, width), lambda i, j: (i, 0, col0 // width + j))

    o_dec, o_dec_ctx = 2 * kt + 2 * vt, kt + vt
    in_specs = [col_spec(t, dk, 0), col_spec(t, dk, kt), col_spec(t, dv, 2 * kt), col_spec(t, dv, 2 * kt + vt),
                col_spec(t, 2 * dk, o_dec), col_spec(t, 2 * dk, o_dec + 2 * kt),
                col_spec(tc, dk, 0), col_spec(tc, dv, kt),
                col_spec(tc, 2 * dk, o_dec_ctx), col_spec(tc, 2 * dk, o_dec_ctx + 2 * kt),
                pl.BlockSpec((1, dv), lambda i, j: (0, 0))]
    tok = lambda width: pltpu.VMEM((t, width), BF16)
    increments = lambda n: [pltpu.VMEM((n, dv, 2 * dk), F32), pltpu.VMEM((n, 1, 2 * dk), F32)]
    return pl.pallas_call(
        functools.partial(_gla_kernel, n_chunks=n_chunks, n_ctx_chunks=n_ctx_chunks, dk=dk, dv=dv,
                          q_scale=float(dk) ** -0.5),
        grid=(b, h),
        in_specs=in_specs,
        out_specs=pl.BlockSpec((1, t, dv), lambda i, j: (i, 0, j)),
        out_shape=jax.ShapeDtypeStruct((b, t, vt), BF16),
        scratch_shapes=[tok(2 * dk), tok(2 * dk), pltpu.VMEM((2 * dk, t), BF16)] + increments(n_chunks)
                       + increments(n_ctx_chunks) + [pltpu.VMEM((n_chunks, 2 * dk, dv), BF16)],
        compiler_params=pltpu.CompilerParams(
            dimension_semantics=("parallel", "parallel"),
            vmem_limit_bytes=_vmem_limit(48 * 1024 * 1024)),
        name="gla_scan",
    )(*([proj] * 6 + [proj_ctx] * 4), head_gain.reshape(1, dv))


def _rolling_steps(n_tiles, head, body):
    s = pl.program_id(0)
    pl.when(s == 0)(head)
    pl.when(jnp.logical_and(s > 0, s < n_tiles))(functools.partial(body, True))
    pl.when(s == n_tiles)(functools.partial(body, False))


def _rolling_slots():
    head_slot = pl.program_id(0) % 2
    return head_slot, 1 - head_slot


def _head_tile(s, n_tiles):
    return jnp.minimum(s, n_tiles - 1)


def _body_tile(s):
    return jnp.maximum(s - 1, 0)


def _shift_rows(u, k):
    z = jnp.zeros((abs(k), u.shape[1]), u.dtype)
    if k > 0:
        return jnp.concatenate([z, u[:-k]], axis=0)
    return jnp.concatenate([u[-k:], z], axis=0)


def _ffn_kernel(x_ref, pre_ref, mix_gate_ref, shift_ref, scale_ref, gate_ref, gain_ref, wmix_ref,
                wup_ref, cw_ref, cb_ref, wdn_ref, fin_ref, o_ref, x_scr, hn_scr, *, hidden, final_norm,
                n_tiles):
    rows_img, cols, d = x_ref.shape[1:]
    tm = rows_img * cols
    head_slot, body_slot = _rolling_slots()

    def head_matmul():
        return jnp.dot(pre_ref[0].reshape(tm, pre_ref.shape[3]), wmix_ref[0], preferred_element_type=F32)

    def head_finish(y):
        x = x_ref[0].reshape(tm, d) + mix_gate_ref[0] * y
        x_scr[head_slot] = x
        hn_scr[head_slot] = _modulated_norm(x, gain_ref[...], shift_ref[0], scale_ref[0]).astype(BF16)

    def conv(u, c0):
        w = cw_ref[:, c0:c0 + HID_CHUNK]
        return (_shift_rows(u, cols) * w[0:1] + u * w[1:2] + _shift_rows(u, -cols) * w[2:3]
                + cb_ref[:, c0:c0 + HID_CHUNK])

    def body(with_head):
        acc = [jnp.zeros((tm, d), F32)]
        head_y = []

        def up_proj(j, _):
            c0 = j * HID_CHUNK
            hn = hn_scr[body_slot]
            u = (jnp.dot(hn, wup_ref[0, :, c0:c0 + HID_CHUNK], preferred_element_type=F32),
                 jnp.dot(hn, wup_ref[0, :, hidden + c0:hidden + c0 + HID_CHUNK], preferred_element_type=F32))
            if with_head and j == 1:
                head_y.append(head_matmul())
            return u

        def gate_down(j, u):
            c0 = j * HID_CHUNK
            act = (conv(u[0], c0) * _silu(conv(u[1], hidden + c0))).astype(BF16)
            acc[0] = acc[0] + jnp.dot(act, wdn_ref[0, c0:c0 + HID_CHUNK, :], preferred_element_type=F32)
            if with_head and j == 1:
                head_finish(head_y.pop())

        _run_staggered([up_proj, gate_down], hidden // HID_CHUNK, newest_first=True)
        out = x_scr[body_slot] + gate_ref[0] * acc[0]
        if final_norm:
            out = out * lax.rsqrt(jnp.mean(out * out, axis=-1, keepdims=True) + EPS) * fin_ref[...]
        o_ref[0] = out.reshape(rows_img, cols, d)

    _rolling_steps(n_tiles, lambda: head_finish(head_matmul()), body)


def _mixer_out_conv_ffn(x, pre, mod, w_mix, gain, w_up, conv_w, conv_b, w_down, fin_gain, *, mix_layer, layer,
                        final_norm):
    b, t, d = x.shape
    hidden = w_down.shape[1]
    assert t % GRID_W == 0 and GRID_W % FFN_COLS == 0 and hidden % HID_CHUNK == 0 and hidden >= 2 * HID_CHUNK
    rows_img = t // GRID_W
    col_tiles = GRID_W // FFN_COLS
    n_tiles = b * col_tiles
    tm = rows_img * FFN_COLS
    x4 = x.reshape(b, rows_img, GRID_W, d)
    pre4 = pre.reshape(b, rows_img, GRID_W, pre.shape[2])
    head = lambda s: _head_tile(s, n_tiles)
    back = _body_tile
    head_tile = lambda width: pl.BlockSpec((1, rows_img, FFN_COLS, width),
                                           lambda s: (head(s) // col_tiles, 0, head(s) % col_tiles, 0))
    head_mod = lambda idx: pl.BlockSpec((1, 1, d), lambda s: (head(s) // col_tiles, 0, idx))
    const = _resident
    out = pl.pallas_call(
        functools.partial(_ffn_kernel, hidden=hidden, final_norm=final_norm, n_tiles=n_tiles),
        grid=(n_tiles + 1,),
        in_specs=[head_tile(d), head_tile(pre.shape[2]), head_mod(2), head_mod(3), head_mod(4),
                  pl.BlockSpec((1, 1, d), lambda s: (back(s) // col_tiles, 0, 5)),
                  const((1, d)), _resident_layer(w_mix.shape, mix_layer), _resident_layer(w_up.shape, layer),
                  const(conv_w.shape), const((1, 2 * hidden)), _resident_layer(w_down.shape, layer),
                  const((1, d))],
        out_specs=pl.BlockSpec((1, rows_img, FFN_COLS, d),
                               lambda s: (back(s) // col_tiles, 0, back(s) % col_tiles, 0)),
        out_shape=jax.ShapeDtypeStruct(x4.shape, F32),
        scratch_shapes=[pltpu.VMEM((2, tm, d), F32), pltpu.VMEM((2, tm, d), BF16)],
        compiler_params=pltpu.CompilerParams(
            dimension_semantics=("arbitrary",),
            vmem_limit_bytes=_vmem_limit(60 * 1024 * 1024)),
        name="conv_ffn",
    )(x4, pre4, mod, mod, mod, mod, gain.reshape(1, d), w_mix, w_up, conv_w, conv_b.reshape(1, 2 * hidden),
      w_down, fin_gain.reshape(1, d))
    return out.reshape(b, t, d)


def _short_conv_kernel(x_ref, shift_ref, scale_ref, gain_ref, win_ref, cw_ref, o_ref, *, width):
    x = x_ref[0]
    tm = x.shape[0]
    hn = _modulated_norm(x, gain_ref[...], shift_ref[0], scale_ref[0]).astype(BF16)
    pos = lax.broadcasted_iota(jnp.int32, (tm, SC_CHUNK), 0) % GRID_W
    has_left = pos != 0
    has_right = pos != GRID_W - 1
    for c0 in range(0, width, SC_CHUNK):
        cg, v, bg = (jnp.dot(hn, win_ref[0, :, base + c0:base + c0 + SC_CHUNK], preferred_element_type=F32)
                     for base in (width, 2 * width, 0))
        u = cg * v
        w = cw_ref[:, c0:c0 + SC_CHUNK]
        left = jnp.where(has_left, pltpu.roll(u, 1, axis=0), 0.0)
        right = jnp.where(has_right, pltpu.roll(u, tm - 1, axis=0), 0.0)
        o_ref[0, :, c0:c0 + SC_CHUNK] = (bg * (left * w[0:1] + u * w[1:2] + right * w[2:3])).astype(o_ref.dtype)


def _short_conv(x, mod, gain, w_in, conv_w, *, layer, tm):
    b, t, d = x.shape
    width = conv_w.shape[1]
    assert t % tm == 0 and tm % GRID_W == 0 and width % SC_CHUNK == 0
    mod_spec = lambda idx: pl.BlockSpec((1, 1, d), lambda i, j: (i, 0, idx))
    const = _resident
    tile = pl.BlockSpec((1, tm, d), lambda i, j: (i, j, 0))
    return pl.pallas_call(
        functools.partial(_short_conv_kernel, width=width),
        grid=(b, t // tm),
        in_specs=[tile, mod_spec(0), mod_spec(1), const((1, d)), _resident_layer(w_in.shape, layer),
                  const(conv_w.shape)],
        out_specs=pl.BlockSpec((1, tm, width), lambda i, j: (i, j, 0)),
        out_shape=jax.ShapeDtypeStruct((b, t, width), BF16),
        compiler_params=pltpu.CompilerParams(
            dimension_semantics=("parallel", "parallel"),
            vmem_limit_bytes=_vmem_limit(56 * 1024 * 1024)),
        name="short_conv",
    )(x, mod, mod, gain.reshape(1, d), w_in, conv_w)


def kernel(x, c, ctx, c_ctx, ada_w, ada_b, norm_mix, norm_ffn, gla_w_in, gla_w_a2, gla_b_a,
           gla_head_norm, gla_w_out, sc_w_in, sc_conv_w, sc_w_out, ffn_w_up, ffn_conv_w,
           ffn_conv_b, ffn_w_down, final_norm):
    b, t, d = x.shape
    depth = ada_w.shape[0]
    assert depth == 2 and gla_w_in.shape[0] == 1 and sc_w_in.shape[0] == 1
    h = GLA_HEADS
    kt, vt, rank = d // 2, d, GLA_GATE_RANK
    dk = kt // h

    pad = (-(b + 1)) % 8
    cvec = jnp.concatenate([c, c_ctx[None, :], jnp.zeros((pad, d), F32)], axis=0)
    mods = _ada_mod(cvec, ada_w, ada_b)
    mod_x = [mods[i, :b].reshape(b, 1, N_MOD * d) for i in range(depth)]
    mod_ctx = mods[0, b:b + 1].reshape(1, 1, N_MOD * d)

    n_in = gla_w_in.shape[2]
    n_pad = (-n_in) % V7X_MXU_DIM
    w_in = jnp.pad(gla_w_in[0], ((0, 0), (0, n_pad))).astype(BF16)
    a2 = gla_w_a2[0].reshape(2, rank, h, dk)
    w2 = jnp.zeros((h, V7X_MXU_DIM, 2 * dk), F32)
    w2 = w2.at[:, :rank, :dk].set(a2[0].transpose(1, 0, 2))
    w2 = w2.at[:, rank:2 * rank, dk:].set(a2[1].transpose(1, 0, 2)).astype(BF16)
    w2 = w2.transpose(1, 0, 2).reshape(V7X_MXU_DIM, 2 * kt)
    ba = gla_b_a[0].reshape(2, h, dk).transpose(1, 0, 2).reshape(1, 2 * kt)

    proj_ctx = _gla_in_proj(ctx, mod_ctx, norm_mix[0], w_in, w2, ba, kt=kt, vt=vt, tm=ctx.shape[1],
                            with_query=False)
    proj_x = _gla_in_proj(x, mod_x[0], norm_mix[0], w_in, w2, ba, kt=kt, vt=vt, tm=TOKEN_TILE, with_query=True)
    o = _gla_scan(proj_x, proj_ctx, gla_head_norm[0], d_model=d)
    w_up, w_down = ffn_w_up.astype(BF16), ffn_w_down.astype(BF16)
    hcur = _mixer_out_conv_ffn(x, o, mod_x[0], gla_w_out.astype(BF16), norm_ffn[0], w_up, ffn_conv_w[0],
                               ffn_conv_b[0], w_down, final_norm, mix_layer=0, layer=0, final_norm=False)

    y = _short_conv(hcur, mod_x[1], norm_mix[1], sc_w_in.astype(BF16), sc_conv_w[0], layer=0, tm=TOKEN_TILE)
    return _mixer_out_conv_ffn(hcur, y, mod_x[1], sc_w_out.astype(BF16), norm_ffn[1], w_up, ffn_conv_w[1],
                               ffn_conv_b[1], w_down, final_norm, mix_layer=0, layer=1, final_norm=True)
```

```python
import functools

import jax
import jax.numpy as jnp
from jax import lax
from jax.experimental import pallas as pl
from jax.experimental.pallas import tpu as pltpu

F32 = jnp.float32
BF16 = jnp.bfloat16

EPS = 1e-6
GRID_W = 64
N_MOD = 6
GLA_HEADS = 4
GLA_GATE_RANK = 16
GLA_GATE_TAU = 16.0
CONV_WIDTH = 3

V7X_LANES = 128
V7X_MXU_DIM = 256
V7X_VMEM_BYTES = 64 * 1024 * 1024

GLA_CHUNK = 128
GLA_STAGE_LAG = 2
TOKEN_TILE = 512
FFN_COLS = 16
HID_CHUNK = 512
SC_CHUNK = 256

TN_DIMS = (((0,), (0,)), ((), ()))


def _vmem_limit(nbytes):
    return int(min(nbytes, V7X_VMEM_BYTES - 4 * 1024 * 1024))


def _resident(shape):
    return pl.BlockSpec(shape, lambda *_: (0,) * len(shape), pipeline_mode=pl.Buffered(1))


def _resident_layer(stacked_shape, layer):
    return pl.BlockSpec((1,) + tuple(stacked_shape[1:]), lambda *_: (layer, 0, 0), pipeline_mode=pl.Buffered(1))


def _modulated_norm(x, gain, shift, scale):
    y = x * lax.rsqrt(jnp.mean(x * x, axis=-1, keepdims=True) + EPS)
    return y * (gain * (1.0 + scale)) + shift


def _silu(x):
    return x * jax.nn.sigmoid(x)


def _run_staggered(stages, n_items, newest_first=False, lag=1):
    handoff = [dict() for _ in stages]
    for t in range(n_items + (len(stages) - 1) * lag):
        order = range(len(stages)) if newest_first else reversed(range(len(stages)))
        for s in order:
            i = t - s * lag
            if 0 <= i < n_items:
                arg = handoff[s - 1].pop(i) if s else None
                handoff[s][i] = stages[s](i, arg)


def _ada_kernel(c_ref, w_ref, b_ref, o_ref):
    c = c_ref[...]
    sc = _silu(c).astype(BF16)
    o_ref[0] = jnp.dot(sc, w_ref[0].astype(BF16), preferred_element_type=F32) + b_ref[0]


def _ada_mod(cvec, ada_w, ada_b):
    depth, d, n = ada_w.shape
    rows = cvec.shape[0]
    tn = n // 4
    return pl.pallas_call(
        _ada_kernel,
        grid=(depth, n // tn),
        in_specs=[
            pl.BlockSpec((rows, d), lambda i, j: (0, 0)),
            pl.BlockSpec((1, d, tn), lambda i, j: (i, 0, j)),
            pl.BlockSpec((1, 1, tn), lambda i, j: (i, 0, j)),
        ],
        out_specs=pl.BlockSpec((1, rows, tn), lambda i, j: (i, 0, j)),
        out_shape=jax.ShapeDtypeStruct((depth, rows, n), F32),
        compiler_params=pltpu.CompilerParams(
            dimension_semantics=("parallel", "parallel"),
            vmem_limit_bytes=_vmem_limit(4 * d * tn * 4)),
        name="ada_mod",
    )(cvec, ada_w, ada_b.reshape(depth, 1, n))


def _cast_weights_once(w_ref, wb_scr):
    n = w_ref.shape[2]
    n_full = n // V7X_LANES * V7X_LANES
    step = 2 * V7X_MXU_DIM

    @pl.when(jnp.logical_and(pl.program_id(0) == 0, pl.program_id(1) == 0))
    def _():
        for c0 in range(0, n_full, step):
            c1 = min(c0 + step, n_full)
            wb_scr[:, c0:c1] = w_ref[0, :, c0:c1].astype(BF16)
        if wb_scr.shape[1] > n_full:
            wb_scr[:, n_full:] = jnp.zeros((wb_scr.shape[0], wb_scr.shape[1] - n_full), BF16)
        if n > n_full:
            wb_scr[:, n_full:n] = w_ref[0, :, n_full:n].astype(BF16)


def _gla_proj_kernel(x_ref, shift_ref, scale_ref, gain_ref, w_ref, w2_ref, ba_ref, o_ref, wb_scr, *, groups,
                     w_low, o_dec):
    _cast_weights_once(w_ref, wb_scr)
    hn = _modulated_norm(x_ref[0], gain_ref[...], shift_ref[0], scale_ref[0]).astype(BF16)
    n_chunk = 2 * V7X_MXU_DIM

    def proj(n0, width):
        return jnp.dot(hn, wb_scr[:, n0:n0 + width], preferred_element_type=F32)

    n_dec = w2_ref.shape[1]
    piece = V7X_MXU_DIM
    a_low = proj(w_low, w2_ref.shape[0]).astype(BF16)

    def log_decay_piece(c0):
        z = jnp.dot(a_low, w2_ref[:, c0:c0 + piece], preferred_element_type=F32) + ba_ref[:, c0:c0 + piece]
        la = (jnp.minimum(z, 0.0) - jnp.log(1.0 + jnp.exp(-jnp.abs(z)))) * (1.0 / GLA_GATE_TAU)
        hi = la.astype(BF16)
        o_ref[0, :, o_dec + c0:o_dec + c0 + piece] = hi
        o_ref[0, :, o_dec + n_dec + c0:o_dec + n_dec + c0 + piece] = (la - hi.astype(F32)).astype(BF16)

    pieces = list(range(0, n_dec, piece))
    for w0, o0, gated in groups:
        y = proj(w0, n_chunk)
        o_ref[0, :, o0:o0 + n_chunk] = (_silu(y) if gated else y).astype(o_ref.dtype)
        if pieces:
            log_decay_piece(pieces.pop(0))
    for c0 in pieces:
        log_decay_piece(c0)


def _gla_in_proj(x, mod, gain, w, w2, ba, *, kt, vt, tm, with_query):
    b, t, d = x.shape
    n_dec = w2.shape[1]
    n_chunk = 2 * V7X_MXU_DIM
    w_low = 2 * kt + 2 * vt
    if with_query:
        groups = ([(c, c, True) for c in range(2 * kt + vt, w_low, n_chunk)]
                  + [(c, c, False) for c in range(0, 2 * kt + vt, n_chunk)])
        o_dec = w_low
    else:
        groups = [(kt + c, c, False) for c in range(0, kt + vt, n_chunk)]
        o_dec = kt + vt
    n_out = o_dec + 2 * n_dec
    per_batch = mod.shape[0] != 1
    mod_map = lambda idx: (lambda i, j: (i if per_batch else 0, 0, idx))
    assert t % tm == 0 and w_low < w.shape[2] <= w_low + w2.shape[0]
    return pl.pallas_call(
        functools.partial(_gla_proj_kernel, groups=tuple(groups), w_low=w_low, o_dec=o_dec),
        grid=(b, t // tm),
        in_specs=[
            pl.BlockSpec((1, tm, d), lambda i, j: (i, j, 0)),
            pl.BlockSpec((1, 1, d), mod_map(0)),
            pl.BlockSpec((1, 1, d), mod_map(1)),
            _resident((1, d)),
            _resident_layer(w.shape, 0),
            _resident(w2.shape),
            _resident(ba.shape),
        ],
        out_specs=pl.BlockSpec((1, tm, n_out), lambda i, j: (i, j, 0)),
        out_shape=jax.ShapeDtypeStruct((b, t, n_out), BF16),
        scratch_shapes=[pltpu.VMEM((d, w_low + w2.shape[0]), BF16)],
        compiler_params=pltpu.CompilerParams(
            dimension_semantics=("arbitrary", "arbitrary"),
            vmem_limit_bytes=_vmem_limit(56 * 1024 * 1024)),
        name="gla_in_proj",
    )(x, mod, mod, gain.reshape(1, d), w, w2, ba)


def _gla_kernel(q_ref, k_ref, v_ref, gs_ref, hi_ref, lo_ref, kc_ref, vc_ref, hic_ref, loc_ref, hg_ref, o_ref,
                qs_scr, qe_scr, kst_scr, pt_scr, et_scr, ptc_scr, etc_scr, sst_scr, *,
                n_chunks, n_ctx_chunks, dk, dv, q_scale):
    c_len = GLA_CHUNK
    mid = c_len // 2

    row = lax.broadcasted_iota(jnp.int32, (c_len, c_len), 0)
    col = lax.broadcasted_iota(jnp.int32, (c_len, c_len), 1)
    lower = row >= col
    upper = row <= col
    tril = jnp.where(lower, 1.0, 0.0).astype(BF16)
    triu = jnp.where(upper, 1.0, 0.0).astype(BF16)

    def rows_of(i):
        return slice(i * c_len, (i + 1) * c_len)

    def prepare_stages(k_ref, v_ref, hi_ref, lo_ref, et_dst, pt_dst, with_query):
        def scaled_operands(i, _):
            rows = rows_of(i)
            hi, lo = hi_ref[0, rows, :], lo_ref[0, rows, :]
            pf = jnp.dot(tril, jnp.concatenate([hi[:, :dk], lo[:, :dk]], axis=1), preferred_element_type=F32)
            pb = jnp.dot(triu, jnp.concatenate([hi[:, dk:], lo[:, dk:]], axis=1), preferred_element_type=F32)
            bf = pf[:, :dk] + pf[:, dk:]
            bb = pb[:, :dk] + pb[:, dk:]
            rf, tf = bf[mid - 1:mid, :], bf[c_len - 1:c_len, :]
            rb, tb = bb[mid:mid + 1, :], bb[0:1, :]
            k = k_ref[0, rows, :].astype(F32)
            dec_f = jnp.exp(bf - rf)
            dec_b = jnp.exp(bb - rb)
            ks_f = k * (1.0 / dec_f)
            ks_b = k * (1.0 / dec_b)
            et_dst[i] = jnp.concatenate([jnp.exp(tf), jnp.exp(tb)], axis=1)
            if with_query:
                q = q_ref[0, rows, :].astype(F32) * q_scale
                qs_f = q * dec_f
                qs_b = q * dec_b
                qs_scr[rows, :] = jnp.concatenate([qs_f, qs_b], axis=1).astype(BF16)
                qe_scr[rows, :] = jnp.concatenate([qs_f * jnp.exp(rf), qs_b * jnp.exp(rb)], axis=1).astype(BF16)
                kst_scr[:dk, rows] = ks_f.T.astype(BF16)
                kst_scr[dk:, rows] = ks_b.T.astype(BF16)
            return jnp.concatenate([ks_f * jnp.exp(tf - rf), ks_b * jnp.exp(tb - rb)], axis=1).astype(BF16)

        def state_increment(i, kd):
            pt_dst[i] = lax.dot_general(v_ref[0, rows_of(i), :], kd, TN_DIMS, preferred_element_type=F32)

        return [scaled_operands, state_increment]

    ctx_stages = prepare_stages(kc_ref, vc_ref, hic_ref, loc_ref, etc_scr, ptc_scr, False)
    lat_stages = prepare_stages(k_ref, v_ref, hi_ref, lo_ref, et_scr, pt_scr, True)
    both = [lambda i, arg, c=c, l=l: c(i, arg) if i < n_ctx_chunks else l(i - n_ctx_chunks, arg)
            for c, l in zip(ctx_stages, lat_stages)]
    _run_staggered(both, n_ctx_chunks + n_chunks, lag=GLA_STAGE_LAG)

    s_f = jnp.zeros((dv, dk), F32)
    s_b = jnp.zeros((dv, dk), F32)
    for c in range(n_ctx_chunks):
        s_f = s_f * etc_scr[c, :, :dk] + ptc_scr[c, :, :dk]
    for c in reversed(range(n_ctx_chunks)):
        s_b = s_b * etc_scr[c, :, dk:] + ptc_scr[c, :, dk:]
    for c in range(n_chunks):
        sst_scr[c, :dk, :] = s_f.T.astype(BF16)
        s_f = s_f * et_scr[c, :, :dk] + pt_scr[c, :, :dk]
    for c in reversed(range(n_chunks)):
        sst_scr[c, dk:, :] = s_b.T.astype(BF16)
        s_b = s_b * et_scr[c, :, dk:] + pt_scr[c, :, dk:]

    def intra_scores(i, _):
        qs = qs_scr[rows_of(i), :]
        att_f = jnp.dot(qs[:, :dk], kst_scr[:dk, rows_of(i)], preferred_element_type=F32)
        att_b = jnp.dot(qs[:, dk:], kst_scr[dk:, rows_of(i)], preferred_element_type=F32)
        return (jnp.where(lower, att_f, 0.0) + jnp.where(upper, att_b, 0.0)).astype(BF16)

    def outputs(i, att):
        rows = rows_of(i)
        o = jnp.dot(att, v_ref[0, rows, :], preferred_element_type=F32)
        o = o + jnp.dot(qe_scr[rows, :], sst_scr[i], preferred_element_type=F32)
        return o, jnp.mean(o * o, axis=-1, keepdims=True)

    def normalise(i, o_ms):
        rows = rows_of(i)
        on = o_ms[0] * lax.rsqrt(o_ms[1] + EPS)
        o_ref[0, rows, :] = ((on * hg_ref[...]) * gs_ref[0, rows, :].astype(F32)).astype(o_ref.dtype)

    _run_staggered([intra_scores, outputs, normalise], n_chunks, lag=GLA_STAGE_LAG)


def _gla_scan(proj, proj_ctx, head_gain, *, d_model):
    b, t, _ = proj.shape
    tc = proj_ctx.shape[1]
    h = GLA_HEADS
    kt, vt = d_model // 2, d_model
    dk, dv = kt // h, vt // h
    assert t % GLA_CHUNK == 0 and tc % GLA_CHUNK == 0
    assert dk == V7X_LANES and dv == V7X_MXU_DIM and 2 * dk == V7X_MXU_DIM
    n_chunks, n_ctx_chunks = t // GLA_CHUNK, tc // GLA_CHUNK

    def col_spec(rows, width, col0):
        return pl.BlockSpec((1, rows, width), lambda i, j: (i, 0, col0 // width + j))

    o_dec, o_dec_ctx = 2 * kt + 2 * vt, kt + vt
    in_specs = [col_spec(t, dk, 0), col_spec(t, dk, kt), col_spec(t, dv, 2 * kt), col_spec(t, dv, 2 * kt + vt),
                col_spec(t, 2 * dk, o_dec), col_spec(t, 2 * dk, o_dec + 2 * kt),
                col_spec(tc, dk, 0), col_spec(tc, dv, kt),
                col_spec(tc, 2 * dk, o_dec_ctx), col_spec(tc, 2 * dk, o_dec_ctx + 2 * kt),
                pl.BlockSpec((1, dv), lambda i, j: (0, 0))]
    tok = lambda width: pltpu.VMEM((t, width), BF16)
    increments = lambda n: [pltpu.VMEM((n, dv, 2 * dk), F32), pltpu.VMEM((n, 1, 2 * dk), F32)]
    return pl.pallas_call(
        functools.partial(_gla_kernel, n_chunks=n_chunks, n_ctx_chunks=n_ctx_chunks, dk=dk, dv=dv,
                          q_scale=float(dk) ** -0.5),
        grid=(b, h),
        in_specs=in_specs,
        out_specs=pl.BlockSpec((1, t, dv), lambda i, j: (i, 0, j)),
        out_shape=jax.ShapeDtypeStruct((b, t, vt), BF16),
        scratch_shapes=[tok(2 * dk), tok(2 * dk), pltpu.VMEM((2 * dk, t), BF16)] + increments(n_chunks)
                       + increments(n_ctx_chunks) + [pltpu.VMEM((n_chunks, 2 * dk, dv), BF16)],
        compiler_params=pltpu.CompilerParams(
            dimension_semantics=("parallel", "parallel"),
            vmem_limit_bytes=_vmem_limit(48 * 1024 * 1024)),
        name="gla_scan",
    )(*([proj] * 6 + [proj_ctx] * 4), head_gain.reshape(1, dv))


def _rolling_steps(n_tiles, head, body):
    s = pl.program_id(0)
    pl.when(s == 0)(head)
    pl.when(jnp.logical_and(s > 0, s < n_tiles))(functools.partial(body, True))
    pl.when(s == n_tiles)(functools.partial(body, False))


def _rolling_slots():
    head_slot = pl.program_id(0) % 2
    return head_slot, 1 - head_slot


def _head_tile(s, n_tiles):
    return jnp.minimum(s, n_tiles - 1)


def _body_tile(s):
    return jnp.maximum(s - 1, 0)


def _shift_rows(u, k):
    z = jnp.zeros((abs(k), u.shape[1]), u.dtype)
    if k > 0:
        return jnp.concatenate([z, u[:-k]], axis=0)
    return jnp.concatenate([u[-k:], z], axis=0)


def _ffn_kernel(x_ref, pre_ref, mix_gate_ref, shift_ref, scale_ref, gate_ref, gain_ref, wmix_ref,
                wup_ref, cw_ref, cb_ref, wdn_ref, fin_ref, o_ref, x_scr, hn_scr, *, hidden, final_norm,
                n_tiles):
    rows_img, cols, d = x_ref.shape[1:]
    tm = rows_img * cols
    head_slot, body_slot = _rolling_slots()

    def head_matmul():
        return jnp.dot(pre_ref[0].reshape(tm, pre_ref.shape[3]), wmix_ref[0], preferred_element_type=F32)

    def head_finish(y):
        x = x_ref[0].reshape(tm, d) + mix_gate_ref[0] * y
        x_scr[head_slot] = x
        hn_scr[head_slot] = _modulated_norm(x, gain_ref[...], shift_ref[0], scale_ref[0]).astype(BF16)

    def conv(u, c0):
        w = cw_ref[:, c0:c0 + HID_CHUNK]
        return (_shift_rows(u, cols) * w[0:1] + u * w[1:2] + _shift_rows(u, -cols) * w[2:3]
                + cb_ref[:, c0:c0 + HID_CHUNK])

    def body(with_head):
        acc = [jnp.zeros((tm, d), F32)]
        head_y = []

        def up_proj(j, _):
            c0 = j * HID_CHUNK
            hn = hn_scr[body_slot]
            u = (jnp.dot(hn, wup_ref[0, :, c0:c0 + HID_CHUNK], preferred_element_type=F32),
                 jnp.dot(hn, wup_ref[0, :, hidden + c0:hidden + c0 + HID_CHUNK], preferred_element_type=F32))
            if with_head and j == 1:
                head_y.append(head_matmul())
            return u

        def gate_down(j, u):
            c0 = j * HID_CHUNK
            act = (conv(u[0], c0) * _silu(conv(u[1], hidden + c0))).astype(BF16)
            acc.append(act)
            if with_head and j == 1:
                head_finish(head_y.pop())

        _run_staggered([up_proj, gate_down], hidden // HID_CHUNK, newest_first=True)
        acc[0] = jnp.dot(jnp.concatenate(acc[1:], axis=1), wdn_ref[0], preferred_element_type=F32)
        out = x_scr[body_slot] + gate_ref[0] * acc[0]
        if final_norm:
            out = out * lax.rsqrt(jnp.mean(out * out, axis=-1, keepdims=True) + EPS) * fin_ref[...]
        o_ref[0] = out.reshape(rows_img, cols, d)

    _rolling_steps(n_tiles, lambda: head_finish(head_matmul()), body)


def _mixer_out_conv_ffn(x, pre, mod, w_mix, gain, w_up, conv_w, conv_b, w_down, fin_gain, *, mix_layer, layer,
                        final_norm):
    b, t, d = x.shape
    hidden = w_down.shape[1]
    assert t % GRID_W == 0 and GRID_W % FFN_COLS == 0 and hidden % HID_CHUNK == 0 and hidden >= 2 * HID_CHUNK
    rows_img = t // GRID_W
    col_tiles = GRID_W // FFN_COLS
    n_tiles = b * col_tiles
    tm = rows_img * FFN_COLS
    x4 = x.reshape(b, rows_img, GRID_W, d)
    pre4 = pre.reshape(b, rows_img, GRID_W, pre.shape[2])
    head = lambda s: _head_tile(s, n_tiles)
    back = _body_tile
    head_tile = lambda width: pl.BlockSpec((1, rows_img, FFN_COLS, width),
                                           lambda s: (head(s) // col_tiles, 0, head(s) % col_tiles, 0))
    head_mod = lambda idx: pl.BlockSpec((1, 1, d), lambda s: (head(s) // col_tiles, 0, idx))
    const = _resident
    out = pl.pallas_call(
        functools.partial(_ffn_kernel, hidden=hidden, final_norm=final_norm, n_tiles=n_tiles),
        grid=(n_tiles + 1,),
        in_specs=[head_tile(d), head_tile(pre.shape[2]), head_mod(2), head_mod(3), head_mod(4),
                  pl.BlockSpec((1, 1, d), lambda s: (back(s) // col_tiles, 0, 5)),
                  const((1, d)), _resident_layer(w_mix.shape, mix_layer), _resident_layer(w_up.shape, layer),
                  const(conv_w.shape), const((1, 2 * hidden)), _resident_layer(w_down.shape, layer),
                  const((1, d))],
        out_specs=pl.BlockSpec((1, rows_img, FFN_COLS, d),
                               lambda s: (back(s) // col_tiles, 0, back(s) % col_tiles, 0)),
        out_shape=jax.ShapeDtypeStruct(x4.shape, F32),
        scratch_shapes=[pltpu.VMEM((2, tm, d), F32), pltpu.VMEM((2, tm, d), BF16)],
        compiler_params=pltpu.CompilerParams(
            dimension_semantics=("arbitrary",),
            vmem_limit_bytes=_vmem_limit(60 * 1024 * 1024)),
        name="conv_ffn",
    )(x4, pre4, mod, mod, mod, mod, gain.reshape(1, d), w_mix, w_up, conv_w, conv_b.reshape(1, 2 * hidden),
      w_down, fin_gain.reshape(1, d))
    return out.reshape(b, t, d)


def _short_conv_kernel(x_ref, shift_ref, scale_ref, gain_ref, win_ref, cw_ref, o_ref, wb_scr, *, width):
    _cast_weights_once(win_ref, wb_scr)
    x = x_ref[0]
    tm = x.shape[0]
    hn = _modulated_norm(x, gain_ref[...], shift_ref[0], scale_ref[0]).astype(BF16)
    pos = lax.broadcasted_iota(jnp.int32, (tm, SC_CHUNK), 0) % GRID_W
    has_left = pos != 0
    has_right = pos != GRID_W - 1
    for c0 in range(0, width, SC_CHUNK):
        cg, v, bg = (jnp.dot(hn, wb_scr[:, base + c0:base + c0 + SC_CHUNK], preferred_element_type=F32)
                     for base in (width, 2 * width, 0))
        u = cg * v
        w = cw_ref[:, c0:c0 + SC_CHUNK]
        left = jnp.where(has_left, pltpu.roll(u, 1, axis=0), 0.0)
        right = jnp.where(has_right, pltpu.roll(u, tm - 1, axis=0), 0.0)
        o_ref[0, :, c0:c0 + SC_CHUNK] = (bg * (left * w[0:1] + u * w[1:2] + right * w[2:3])).astype(o_ref.dtype)


def _short_conv(x, mod, gain, w_in, conv_w, *, layer, tm):
    b, t, d = x.shape
    width = conv_w.shape[1]
    assert t % tm == 0 and tm % GRID_W == 0 and width % SC_CHUNK == 0
    mod_spec = lambda idx: pl.BlockSpec((1, 1, d), lambda i, j: (i, 0, idx))
    const = _resident
    tile = pl.BlockSpec((1, tm, d), lambda i, j: (i, j, 0))
    return pl.pallas_call(
        functools.partial(_short_conv_kernel, width=width),
        grid=(b, t // tm),
        in_specs=[tile, mod_spec(0), mod_spec(1), const((1, d)), _resident_layer(w_in.shape, layer),
                  const(conv_w.shape)],
        out_specs=pl.BlockSpec((1, tm, width), lambda i, j: (i, j, 0)),
        out_shape=jax.ShapeDtypeStruct((b, t, width), BF16),
        scratch_shapes=[pltpu.VMEM(w_in.shape[1:], BF16)],
        compiler_params=pltpu.CompilerParams(
            dimension_semantics=("arbitrary", "arbitrary"),
            vmem_limit_bytes=_vmem_limit(56 * 1024 * 1024)),
        name="short_conv",
    )(x, mod, mod, gain.reshape(1, d), w_in, conv_w)


def kernel(x, c, ctx, c_ctx, ada_w, ada_b, norm_mix, norm_ffn, gla_w_in, gla_w_a2, gla_b_a,
           gla_head_norm, gla_w_out, sc_w_in, sc_conv_w, sc_w_out, ffn_w_up, ffn_conv_w,
           ffn_conv_b, ffn_w_down, final_norm):
    b, t, d = x.shape
    depth = ada_w.shape[0]
    assert depth == 2 and gla_w_in.shape[0] == 1 and sc_w_in.shape[0] == 1
    h = GLA_HEADS
    kt, vt, rank = d // 2, d, GLA_GATE_RANK
    dk = kt // h

    pad = (-(b + 1)) % 8
    cvec = jnp.concatenate([c, c_ctx[None, :], jnp.zeros((pad, d), F32)], axis=0)
    mods = _ada_mod(cvec, ada_w, ada_b)
    mod_x = [mods[i, :b].reshape(b, 1, N_MOD * d) for i in range(depth)]
    mod_ctx = mods[0, b:b + 1].reshape(1, 1, N_MOD * d)

    a2 = gla_w_a2[0].reshape(2, rank, h, dk)
    w2 = jnp.zeros((h, V7X_MXU_DIM, 2 * dk), F32)
    w2 = w2.at[:, :rank, :dk].set(a2[0].transpose(1, 0, 2))
    w2 = w2.at[:, rank:2 * rank, dk:].set(a2[1].transpose(1, 0, 2)).astype(BF16)
    w2 = w2.transpose(1, 0, 2).reshape(V7X_MXU_DIM, 2 * kt)
    ba = gla_b_a[0].reshape(2, h, dk).transpose(1, 0, 2).reshape(1, 2 * kt)

    proj_ctx = _gla_in_proj(ctx, mod_ctx, norm_mix[0], gla_w_in, w2, ba, kt=kt, vt=vt, tm=ctx.shape[1],
                            with_query=False)
    proj_x = _gla_in_proj(x, mod_x[0], norm_mix[0], gla_w_in, w2, ba, kt=kt, vt=vt, tm=TOKEN_TILE,
                          with_query=True)
    o = _gla_scan(proj_x, proj_ctx, gla_head_norm[0], d_model=d)
    w_up, w_down = ffn_w_up.astype(BF16), ffn_w_down.astype(BF16)
    hcur = _mixer_out_conv_ffn(x, o, mod_x[0], gla_w_out.astype(BF16), norm_ffn[0], w_up, ffn_conv_w[0],
                               ffn_conv_b[0], w_down, final_norm, mix_layer=0, layer=0, final_norm=False)

    y = _short_conv(hcur, mod_x[1], norm_mix[1], sc_w_in, sc_conv_w[0], layer=0, tm=TOKEN_TILE)
    return _mixer_out_conv_ffn(hcur, y, mod_x[1], sc_w_out.astype(BF16), norm_ffn[1], w_up, ffn_conv_w[1],
                               ffn_conv_b[1], w_down, final_norm, mix_layer=0, layer=1, final_norm=True)
```

```python
import functools

import jax
import jax.numpy as jnp
from jax import lax
from jax.experimental import pallas as pl
from jax.experimental.pallas import tpu as pltpu

F32 = jnp.float32
BF16 = jnp.bfloat16

EPS = 1e-6
GRID_W = 64
N_MOD = 6
GLA_HEADS = 4
GLA_GATE_RANK = 16
GLA_GATE_TAU = 16.0
CONV_WIDTH = 3

V7X_LANES = 128
V7X_MXU_DIM = 256
V7X_VMEM_BYTES = 64 * 1024 * 1024

GLA_CHUNK = 128
GLA_STAGE_LAG = 2
TOKEN_TILE = 512
FFN_COLS = 16
HID_CHUNK = 512
SC_CHUNK = 256

TN_DIMS = (((0,), (0,)), ((), ()))


def _vmem_limit(nbytes):
    return int(min(nbytes, V7X_VMEM_BYTES - 4 * 1024 * 1024))


def _resident(shape):
    return pl.BlockSpec(shape, lambda *_: (0,) * len(shape), pipeline_mode=pl.Buffered(1))


def _resident_layer(stacked_shape, layer):
    return pl.BlockSpec((1,) + tuple(stacked_shape[1:]), lambda *_: (layer, 0, 0), pipeline_mode=pl.Buffered(1))


def _modulated_norm(x, gain, shift, scale):
    y = x * lax.rsqrt(jnp.mean(x * x, axis=-1, keepdims=True) + EPS)
    return y * (gain * (1.0 + scale)) + shift


def _silu(x):
    return x * jax.nn.sigmoid(x)


def _run_staggered(stages, n_items, newest_first=False, lag=1):
    handoff = [dict() for _ in stages]
    for t in range(n_items + (len(stages) - 1) * lag):
        order = range(len(stages)) if newest_first else reversed(range(len(stages)))
        for s in order:
            i = t - s * lag
            if 0 <= i < n_items:
                arg = handoff[s - 1].pop(i) if s else None
                handoff[s][i] = stages[s](i, arg)


def _ada_kernel(c_ref, w_ref, b_ref, o_ref):
    c = c_ref[...]
    sc = _silu(c).astype(BF16)
    o_ref[0] = jnp.dot(sc, w_ref[0].astype(BF16), preferred_element_type=F32) + b_ref[0]


def _ada_mod(cvec, ada_w, ada_b):
    depth, d, n = ada_w.shape
    rows = cvec.shape[0]
    tn = n // 4
    return pl.pallas_call(
        _ada_kernel,
        grid=(depth, n // tn),
        in_specs=[
            pl.BlockSpec((rows, d), lambda i, j: (0, 0)),
            pl.BlockSpec((1, d, tn), lambda i, j: (i, 0, j)),
            pl.BlockSpec((1, 1, tn), lambda i, j: (i, 0, j)),
        ],
        out_specs=pl.BlockSpec((1, rows, tn), lambda i, j: (i, 0, j)),
        out_shape=jax.ShapeDtypeStruct((depth, rows, n), F32),
        compiler_params=pltpu.CompilerParams(
            dimension_semantics=("parallel", "parallel"),
            vmem_limit_bytes=_vmem_limit(4 * d * tn * 4)),
        name="ada_mod",
    )(cvec, ada_w, ada_b.reshape(depth, 1, n))


def _gla_proj_kernel(x_ref, shift_ref, scale_ref, gain_ref, w_ref, w2_ref, ba_ref, o_ref, *, groups, w_low,
                     o_dec):
    hn = _modulated_norm(x_ref[0], gain_ref[...], shift_ref[0], scale_ref[0]).astype(BF16)
    n_chunk = 2 * V7X_MXU_DIM

    def proj(n0, width):
        return jnp.dot(hn, w_ref[:, n0:n0 + width], preferred_element_type=F32)

    n_dec = w2_ref.shape[1]
    piece = V7X_MXU_DIM
    a_low = proj(w_low, w2_ref.shape[0]).astype(BF16)

    def log_decay_piece(head):
        c0 = head * piece
        z = jnp.dot(a_low, w2_ref[:, c0:c0 + piece], preferred_element_type=F32) + ba_ref[:, c0:c0 + piece]
        la = (jnp.minimum(z, 0.0) - jnp.log(1.0 + jnp.exp(-jnp.abs(z)))) * (1.0 / GLA_GATE_TAU)
        hi = la.astype(BF16)
        o_ref[0, head, :, o_dec:o_dec + piece] = hi
        o_ref[0, head, :, o_dec + piece:o_dec + 2 * piece] = (la - hi.astype(F32)).astype(BF16)

    pieces = list(range(n_dec // piece))
    for w0, o0, width, head0, gated in groups:
        y = proj(w0, n_chunk)
        y = (_silu(y) if gated else y).astype(o_ref.dtype)
        for i in range(n_chunk // width):
            o_ref[0, head0 + i, :, o0:o0 + width] = y[:, i * width:(i + 1) * width]
        if pieces:
            log_decay_piece(pieces.pop(0))
    for head in pieces:
        log_decay_piece(head)


def _gla_in_proj(x, mod, gain, w, w2, ba, *, kt, vt, tm, with_query):
    b, t, d = x.shape
    h = GLA_HEADS
    dk, dv = kt // h, vt // h
    n_chunk = 2 * V7X_MXU_DIM
    w_low = 2 * kt + 2 * vt
    fields = [(kt, kt, dk, False), (2 * kt, vt, dv, False)]
    if with_query:
        fields = [(0, kt, dk, False)] + fields + [(2 * kt + vt, vt, dv, True)]
    groups, o0 = [], 0
    for w0, width, per_head, gated in fields:
        assert width % n_chunk == 0 and n_chunk % per_head == 0
        groups += [(w0 + c, o0, per_head, c // per_head, gated) for c in range(0, width, n_chunk)]
        o0 += per_head
    groups.sort(key=lambda g: not g[4])
    o_dec, n_out = o0, o0 + 4 * dk
    per_batch = mod.shape[0] != 1
    mod_map = lambda idx: (lambda i, j: (i if per_batch else 0, 0, idx))
    assert t % tm == 0 and w.shape[1] == w_low + w2.shape[0] and w2.shape[1] == 2 * kt and 2 * dk == V7X_MXU_DIM
    return pl.pallas_call(
        functools.partial(_gla_proj_kernel, groups=tuple(groups), w_low=w_low, o_dec=o_dec),
        grid=(b, t // tm),
        in_specs=[
            pl.BlockSpec((1, tm, d), lambda i, j: (i, j, 0)),
            pl.BlockSpec((1, 1, d), mod_map(0)),
            pl.BlockSpec((1, 1, d), mod_map(1)),
            _resident((1, d)),
            _resident(w.shape),
            _resident(w2.shape),
            _resident(ba.shape),
        ],
        out_specs=pl.BlockSpec((1, h, tm, n_out), lambda i, j: (i, 0, j, 0)),
        out_shape=jax.ShapeDtypeStruct((b, h, t, n_out), BF16),
        compiler_params=pltpu.CompilerParams(
            dimension_semantics=("parallel", "parallel"),
            vmem_limit_bytes=_vmem_limit(48 * 1024 * 1024)),
        name="gla_in_proj",
    )(x, mod, mod, gain.reshape(1, d), w, w2, ba)


def _gla_kernel(p_ref, pc_ref, hg_ref, o_ref, qs_scr, qe_scr, kst_scr, pt_scr, et_scr, ptc_scr, etc_scr, sst_scr,
                *, n_chunks, n_ctx_chunks, dk, dv, q_scale):
    c_len = GLA_CHUNK
    mid = c_len // 2

    def columns(ref, widths):
        starts = [sum(widths[:i]) for i in range(len(widths))]
        return [lambda rows, c0=c0, w=w: ref[0, 0, rows, c0:c0 + w] for c0, w in zip(starts, widths)]

    q_at, k_at, v_at, gs_at, hi_at, lo_at = columns(p_ref, [dk, dk, dv, dv, 2 * dk, 2 * dk])
    kc_at, vc_at, hic_at, loc_at = columns(pc_ref, [dk, dv, 2 * dk, 2 * dk])

    row = lax.broadcasted_iota(jnp.int32, (c_len, c_len), 0)
    col = lax.broadcasted_iota(jnp.int32, (c_len, c_len), 1)
    lower = row >= col
    upper = row <= col
    tril = jnp.where(lower, 1.0, 0.0).astype(BF16)
    triu = jnp.where(upper, 1.0, 0.0).astype(BF16)

    def rows_of(i):
        return slice(i * c_len, (i + 1) * c_len)

    def prepare_stages(k_at, v_at, hi_at, lo_at, et_dst, pt_dst, with_query):
        def scaled_operands(i, _):
            rows = rows_of(i)
            hi, lo = hi_at(rows), lo_at(rows)
            pf = jnp.dot(tril, jnp.concatenate([hi[:, :dk], lo[:, :dk]], axis=1), preferred_element_type=F32)
            pb = jnp.dot(triu, jnp.concatenate([hi[:, dk:], lo[:, dk:]], axis=1), preferred_element_type=F32)
            bf = pf[:, :dk] + pf[:, dk:]
            bb = pb[:, :dk] + pb[:, dk:]
            rf, tf = bf[mid - 1:mid, :], bf[c_len - 1:c_len, :]
            rb, tb = bb[mid:mid + 1, :], bb[0:1, :]
            k = k_at(rows).astype(F32)
            dec_f = jnp.exp(bf - rf)
            dec_b = jnp.exp(bb - rb)
            ks_f = k * (1.0 / dec_f)
            ks_b = k * (1.0 / dec_b)
            et_dst[i] = jnp.concatenate([jnp.exp(tf), jnp.exp(tb)], axis=1)
            if with_query:
                q = q_at(rows).astype(F32) * q_scale
                qs_f = q * dec_f
                qs_b = q * dec_b
                qs_scr[rows, :] = jnp.concatenate([qs_f, qs_b], axis=1).astype(BF16)
                qe_scr[rows, :] = jnp.concatenate([qs_f * jnp.exp(rf), qs_b * jnp.exp(rb)], axis=1).astype(BF16)
                kst_scr[:dk, rows] = ks_f.T.astype(BF16)
                kst_scr[dk:, rows] = ks_b.T.astype(BF16)
            return jnp.concatenate([ks_f * jnp.exp(tf - rf), ks_b * jnp.exp(tb - rb)], axis=1).astype(BF16)

        def state_increment(i, kd):
            pt_dst[i] = lax.dot_general(v_at(rows_of(i)), kd, TN_DIMS, preferred_element_type=F32)

        return [scaled_operands, state_increment]

    ctx_stages = prepare_stages(kc_at, vc_at, hic_at, loc_at, etc_scr, ptc_scr, False)
    lat_stages = prepare_stages(k_at, v_at, hi_at, lo_at, et_scr, pt_scr, True)
    both = [lambda i, arg, c=c, l=l: c(i, arg) if i < n_ctx_chunks else l(i - n_ctx_chunks, arg)
            for c, l in zip(ctx_stages, lat_stages)]
    _run_staggered(both, n_ctx_chunks + n_chunks, lag=GLA_STAGE_LAG)

    s_f = jnp.zeros((dv, dk), F32)
    s_b = jnp.zeros((dv, dk), F32)
    for c in range(n_ctx_chunks):
        s_f = s_f * etc_scr[c, :, :dk] + ptc_scr[c, :, :dk]
    for c in reversed(range(n_ctx_chunks)):
        s_b = s_b * etc_scr[c, :, dk:] + ptc_scr[c, :, dk:]
    for c in range(n_chunks):
        sst_scr[c, :dk, :] = s_f.T.astype(BF16)
        s_f = s_f * et_scr[c, :, :dk] + pt_scr[c, :, :dk]
    for c in reversed(range(n_chunks)):
        sst_scr[c, dk:, :] = s_b.T.astype(BF16)
        s_b = s_b * et_scr[c, :, dk:] + pt_scr[c, :, dk:]

    def intra_scores(i, _):
        qs = qs_scr[rows_of(i), :]
        att_f = jnp.dot(qs[:, :dk], kst_scr[:dk, rows_of(i)], preferred_element_type=F32)
        att_b = jnp.dot(qs[:, dk:], kst_scr[dk:, rows_of(i)], preferred_element_type=F32)
        return (jnp.where(lower, att_f, 0.0) + jnp.where(upper, att_b, 0.0)).astype(BF16)

    def outputs(i, att):
        rows = rows_of(i)
        o = jnp.dot(att, v_at(rows), preferred_element_type=F32)
        o = o + jnp.dot(qe_scr[rows, :], sst_scr[i], preferred_element_type=F32)
        return o, jnp.mean(o * o, axis=-1, keepdims=True)

    def normalise(i, o_ms):
        rows = rows_of(i)
        on = o_ms[0] * lax.rsqrt(o_ms[1] + EPS)
        o_ref[0, rows, :] = ((on * hg_ref[...]) * gs_at(rows).astype(F32)).astype(o_ref.dtype)

    _run_staggered([intra_scores, outputs, normalise], n_chunks, lag=GLA_STAGE_LAG)


def _gla_scan(proj, proj_ctx, head_gain, *, d_model):
    b, h, t, _ = proj.shape
    tc = proj_ctx.shape[2]
    kt, vt = d_model // 2, d_model
    dk, dv = kt // h, vt // h
    assert h == GLA_HEADS and t % GLA_CHUNK == 0 and tc % GLA_CHUNK == 0
    assert dk == V7X_LANES and dv == V7X_MXU_DIM and 2 * dk == V7X_MXU_DIM
    n_chunks, n_ctx_chunks = t // GLA_CHUNK, tc // GLA_CHUNK
    head_block = lambda a: pl.BlockSpec((1, 1) + a.shape[2:], lambda i, j: (i, j, 0, 0))
    in_specs = [head_block(proj), head_block(proj_ctx), pl.BlockSpec((1, dv), lambda i, j: (0, 0))]
    tok = lambda width: pltpu.VMEM((t, width), BF16)
    increments = lambda n: [pltpu.VMEM((n, dv, 2 * dk), F32), pltpu.VMEM((n, 1, 2 * dk), F32)]
    return pl.pallas_call(
        functools.partial(_gla_kernel, n_chunks=n_chunks, n_ctx_chunks=n_ctx_chunks, dk=dk, dv=dv,
                          q_scale=float(dk) ** -0.5),
        grid=(b, h),
        in_specs=in_specs,
        out_specs=pl.BlockSpec((1, t, dv), lambda i, j: (i, 0, j)),
        out_shape=jax.ShapeDtypeStruct((b, t, vt), BF16),
        scratch_shapes=[tok(2 * dk), tok(2 * dk), pltpu.VMEM((2 * dk, t), BF16)] + increments(n_chunks)
                       + increments(n_ctx_chunks) + [pltpu.VMEM((n_chunks, 2 * dk, dv), BF16)],
        compiler_params=pltpu.CompilerParams(
            dimension_semantics=("parallel", "parallel"),
            vmem_limit_bytes=_vmem_limit(48 * 1024 * 1024)),
        name="gla_scan",
    )(proj, proj_ctx, head_gain.reshape(1, dv))


def _rolling_steps(n_tiles, head, body):
    s = pl.program_id(0)
    pl.when(s == 0)(head)
    pl.when(jnp.logical_and(s > 0, s < n_tiles))(functools.partial(body, True))
    pl.when(s == n_tiles)(functools.partial(body, False))


def _rolling_slots():
    head_slot = pl.program_id(0) % 2
    return head_slot, 1 - head_slot


def _head_tile(s, n_tiles):
    return jnp.minimum(s, n_tiles - 1)


def _body_tile(s):
    return jnp.maximum(s - 1, 0)


def _shift_rows(u, k):
    z = jnp.zeros((abs(k), u.shape[1]), u.dtype)
    if k > 0:
        return jnp.concatenate([z, u[:-k]], axis=0)
    return jnp.concatenate([u[-k:], z], axis=0)


def _ffn_kernel(x_ref, pre_ref, mix_gate_ref, shift_ref, scale_ref, gate_ref, gain_ref, wmix_ref,
                wup_ref, cw_ref, cb_ref, wdn_ref, fin_ref, o_ref, x_scr, hn_scr, *, hidden, final_norm,
                n_tiles):
    rows_img, cols, d = x_ref.shape[1:]
    tm = rows_img * cols
    head_slot, body_slot = _rolling_slots()

    def head_matmul():
        return jnp.dot(pre_ref[0].reshape(tm, pre_ref.shape[3]), wmix_ref[0], preferred_element_type=F32)

    def head_finish(y):
        x = x_ref[0].reshape(tm, d) + mix_gate_ref[0] * y
        x_scr[head_slot] = x
        hn_scr[head_slot] = _modulated_norm(x, gain_ref[...], shift_ref[0], scale_ref[0]).astype(BF16)

    def conv(u, c0):
        w = cw_ref[:, c0:c0 + HID_CHUNK]
        return (_shift_rows(u, cols) * w[0:1] + u * w[1:2] + _shift_rows(u, -cols) * w[2:3]
                + cb_ref[:, c0:c0 + HID_CHUNK])

    def body(with_head):
        acts = []
        head_y = []

        def up_proj(j, _):
            c0 = j * HID_CHUNK
            hn = hn_scr[body_slot]
            u = (jnp.dot(hn, wup_ref[0, :, c0:c0 + HID_CHUNK], preferred_element_type=F32),
                 jnp.dot(hn, wup_ref[0, :, hidden + c0:hidden + c0 + HID_CHUNK], preferred_element_type=F32))
            if with_head and j == 1:
                head_y.append(head_matmul())
            return u

        def conv_gate(j, u):
            c0 = j * HID_CHUNK
            acts.append((conv(u[0], c0) * _silu(conv(u[1], hidden + c0))).astype(BF16))
            if with_head and j == 1:
                head_finish(head_y.pop())

        _run_staggered([up_proj, conv_gate], hidden // HID_CHUNK, newest_first=True)
        down = jnp.dot(jnp.concatenate(acts, axis=1), wdn_ref[0], preferred_element_type=F32)
        out = x_scr[body_slot] + gate_ref[0] * down
        if final_norm:
            out = out * lax.rsqrt(jnp.mean(out * out, axis=-1, keepdims=True) + EPS) * fin_ref[...]
        o_ref[0] = out.reshape(rows_img, cols, d)

    _rolling_steps(n_tiles, lambda: head_finish(head_matmul()), body)


def _mixer_out_conv_ffn(x, pre, mod, w_mix, gain, w_up, conv_w, conv_b, w_down, fin_gain, *, mix_layer, layer,
                        final_norm):
    b, t, d = x.shape
    hidden = w_down.shape[1]
    assert t % GRID_W == 0 and GRID_W % FFN_COLS == 0 and hidden % HID_CHUNK == 0 and hidden >= 2 * HID_CHUNK
    rows_img = t // GRID_W
    col_tiles = GRID_W // FFN_COLS
    n_tiles = b * col_tiles
    tm = rows_img * FFN_COLS
    x4 = x.reshape(b, rows_img, GRID_W, d)
    pre4 = pre.reshape(b, rows_img, GRID_W, pre.shape[2])
    head = lambda s: _head_tile(s, n_tiles)
    back = _body_tile
    head_tile = lambda width: pl.BlockSpec((1, rows_img, FFN_COLS, width),
                                           lambda s: (head(s) // col_tiles, 0, head(s) % col_tiles, 0))
    head_mod = lambda idx: pl.BlockSpec((1, 1, d), lambda s: (head(s) // col_tiles, 0, idx))
    const = _resident
    out = pl.pallas_call(
        functools.partial(_ffn_kernel, hidden=hidden, final_norm=final_norm, n_tiles=n_tiles),
        grid=(n_tiles + 1,),
        in_specs=[head_tile(d), head_tile(pre.shape[2]), head_mod(2), head_mod(3), head_mod(4),
                  pl.BlockSpec((1, 1, d), lambda s: (back(s) // col_tiles, 0, 5)),
                  const((1, d)), _resident_layer(w_mix.shape, mix_layer), _resident_layer(w_up.shape, layer),
                  const(conv_w.shape), const((1, 2 * hidden)), _resident_layer(w_down.shape, layer),
                  const((1, d))],
        out_specs=pl.BlockSpec((1, rows_img, FFN_COLS, d),
                               lambda s: (back(s) // col_tiles, 0, back(s) % col_tiles, 0)),
        out_shape=jax.ShapeDtypeStruct(x4.shape, F32),
        scratch_shapes=[pltpu.VMEM((2, tm, d), F32), pltpu.VMEM((2, tm, d), BF16)],
        compiler_params=pltpu.CompilerParams(
            dimension_semantics=("arbitrary",),
            vmem_limit_bytes=_vmem_limit(60 * 1024 * 1024)),
        name="conv_ffn",
    )(x4, pre4, mod, mod, mod, mod, gain.reshape(1, d), w_mix, w_up, conv_w, conv_b.reshape(1, 2 * hidden),
      w_down, fin_gain.reshape(1, d))
    return out.reshape(b, t, d)


def _short_conv_kernel(x_ref, shift_ref, scale_ref, gain_ref, win_ref, cw_ref, o_ref, *, width):
    x = x_ref[0]
    tm = x.shape[0]
    hn = _modulated_norm(x, gain_ref[...], shift_ref[0], scale_ref[0]).astype(BF16)
    pos = lax.broadcasted_iota(jnp.int32, (tm, SC_CHUNK), 0) % GRID_W
    has_left = pos != 0
    has_right = pos != GRID_W - 1
    for c0 in range(0, width, SC_CHUNK):
        cg, v, bg = (jnp.dot(hn, win_ref[0, :, base + c0:base + c0 + SC_CHUNK], preferred_element_type=F32)
                     for base in (width, 2 * width, 0))
        u = cg * v
        w = cw_ref[:, c0:c0 + SC_CHUNK]
        left = jnp.where(has_left, pltpu.roll(u, 1, axis=0), 0.0)
        right = jnp.where(has_right, pltpu.roll(u, tm - 1, axis=0), 0.0)
        o_ref[0, :, c0:c0 + SC_CHUNK] = (bg * (left * w[0:1] + u * w[1:2] + right * w[2:3])).astype(o_ref.dtype)


def _short_conv(x, mod, gain, w_in, conv_w, *, layer, tm):
    b, t, d = x.shape
    width = conv_w.shape[1]
    assert t % tm == 0 and tm % GRID_W == 0 and width % SC_CHUNK == 0
    mod_spec = lambda idx: pl.BlockSpec((1, 1, d), lambda i, j: (i, 0, idx))
    const = _resident
    tile = pl.BlockSpec((1, tm, d), lambda i, j: (i, j, 0))
    return pl.pallas_call(
        functools.partial(_short_conv_kernel, width=width),
        grid=(b, t // tm),
        in_specs=[tile, mod_spec(0), mod_spec(1), const((1, d)), _resident_layer(w_in.shape, layer),
                  const(conv_w.shape)],
        out_specs=pl.BlockSpec((1, tm, width), lambda i, j: (i, j, 0)),
        out_shape=jax.ShapeDtypeStruct((b, t, width), BF16),
        compiler_params=pltpu.CompilerParams(
            dimension_semantics=("parallel", "parallel"),
            vmem_limit_bytes=_vmem_limit(56 * 1024 * 1024)),
        name="short_conv",
    )(x, mod, mod, gain.reshape(1, d), w_in, conv_w)


def kernel(x, c, ctx, c_ctx, ada_w, ada_b, norm_mix, norm_ffn, gla_w_in, gla_w_a2, gla_b_a,
           gla_head_norm, gla_w_out, sc_w_in, sc_conv_w, sc_w_out, ffn_w_up, ffn_conv_w,
           ffn_conv_b, ffn_w_down, final_norm):
    b, t, d = x.shape
    depth = ada_w.shape[0]
    assert depth == 2 and gla_w_in.shape[0] == 1 and sc_w_in.shape[0] == 1
    h = GLA_HEADS
    kt, vt, rank = d // 2, d, GLA_GATE_RANK
    dk = kt // h

    pad = (-(b + 1)) % 8
    cvec = jnp.concatenate([c, c_ctx[None, :], jnp.zeros((pad, d), F32)], axis=0)
    mods = _ada_mod(cvec, ada_w, ada_b)
    mod_x = [mods[i, :b].reshape(b, 1, N_MOD * d) for i in range(depth)]
    mod_ctx = mods[0, b:b + 1].reshape(1, 1, N_MOD * d)

    n_in = gla_w_in.shape[2]
    n_pad = (-n_in) % V7X_MXU_DIM
    w_in = jnp.pad(gla_w_in[0], ((0, 0), (0, n_pad))).astype(BF16)
    a2 = gla_w_a2[0].reshape(2, rank, h, dk)
    w2 = jnp.zeros((h, V7X_MXU_DIM, 2 * dk), F32)
    w2 = w2.at[:, :rank, :dk].set(a2[0].transpose(1, 0, 2))
    w2 = w2.at[:, rank:2 * rank, dk:].set(a2[1].transpose(1, 0, 2)).astype(BF16)
    w2 = w2.transpose(1, 0, 2).reshape(V7X_MXU_DIM, 2 * kt)
    ba = gla_b_a[0].reshape(2, h, dk).transpose(1, 0, 2).reshape(1, 2 * kt)

    proj_ctx = _gla_in_proj(ctx, mod_ctx, norm_mix[0], w_in, w2, ba, kt=kt, vt=vt, tm=ctx.shape[1],
                            with_query=False)
    proj_x = _gla_in_proj(x, mod_x[0], norm_mix[0], w_in, w2, ba, kt=kt, vt=vt, tm=TOKEN_TILE, with_query=True)
    o = _gla_scan(proj_x, proj_ctx, gla_head_norm[0], d_model=d)
    w_up, w_down = ffn_w_up.astype(BF16), ffn_w_down.astype(BF16)
    hcur = _mixer_out_conv_ffn(x, o, mod_x[0], gla_w_out.astype(BF16), norm_ffn[0], w_up, ffn_conv_w[0],
                               ffn_conv_b[0], w_down, final_norm, mix_layer=0, layer=0, final_norm=False)

    y = _short_conv(hcur, mod_x[1], norm_mix[1], sc_w_in.astype(BF16), sc_conv_w[0], layer=0, tm=TOKEN_TILE)
    return _mixer_out_conv_ffn(hcur, y, mod_x[1], sc_w_out.astype(BF16), norm_ffn[1], w_up, ffn_conv_w[1],
                               ffn_conv_b[1], w_down, final_norm, mix_layer=0, layer=1, final_norm=True)
```

```python
import functools

import jax
import jax.numpy as jnp
from jax import lax
from jax.experimental import pallas as pl
from jax.experimental.pallas import tpu as pltpu

F32 = jnp.float32
BF16 = jnp.bfloat16

EPS = 1e-6
GRID_W = 64
N_MOD = 6
GLA_HEADS = 4
GLA_GATE_RANK = 16
GLA_GATE_TAU = 16.0
CONV_WIDTH = 3

V7X_LANES = 128
V7X_MXU_DIM = 256
V7X_VMEM_BYTES = 64 * 1024 * 1024

GLA_CHUNK = 128
GLA_STAGE_LAG = 2
TOKEN_TILE = 512
FFN_COLS = 16
HID_CHUNK = 512
SC_CHUNK = 256

TN_DIMS = (((0,), (0,)), ((), ()))


def _vmem_limit(nbytes):
    return int(min(nbytes, V7X_VMEM_BYTES - 4 * 1024 * 1024))


def _resident(shape):
    return pl.BlockSpec(shape, lambda *_: (0,) * len(shape), pipeline_mode=pl.Buffered(1))


def _resident_layer(stacked_shape, layer):
    return pl.BlockSpec((1,) + tuple(stacked_shape[1:]), lambda *_: (layer, 0, 0), pipeline_mode=pl.Buffered(1))


def _modulated_norm(x, gain, shift, scale):
    y = x * lax.rsqrt(jnp.mean(x * x, axis=-1, keepdims=True) + EPS)
    return y * (gain * (1.0 + scale)) + shift


def _silu(x):
    return x * jax.nn.sigmoid(x)


def _run_staggered(stages, n_items, newest_first=False, lag=1):
    handoff = [dict() for _ in stages]
    for t in range(n_items + (len(stages) - 1) * lag):
        order = range(len(stages)) if newest_first else reversed(range(len(stages)))
        for s in order:
            i = t - s * lag
            if 0 <= i < n_items:
                arg = handoff[s - 1].pop(i) if s else None
                handoff[s][i] = stages[s](i, arg)


def _ada_kernel(c_ref, w_ref, b_ref, o_ref):
    c = c_ref[...]
    sc = _silu(c).astype(BF16)
    o_ref[0] = jnp.dot(sc, w_ref[0].astype(BF16), preferred_element_type=F32) + b_ref[0]


def _ada_mod(cvec, ada_w, ada_b):
    depth, d, n = ada_w.shape
    rows = cvec.shape[0]
    tn = n // 4
    return pl.pallas_call(
        _ada_kernel,
        grid=(depth, n // tn),
        in_specs=[
            pl.BlockSpec((rows, d), lambda i, j: (0, 0)),
            pl.BlockSpec((1, d, tn), lambda i, j: (i, 0, j)),
            pl.BlockSpec((1, 1, tn), lambda i, j: (i, 0, j)),
        ],
        out_specs=pl.BlockSpec((1, rows, tn), lambda i, j: (i, 0, j)),
        out_shape=jax.ShapeDtypeStruct((depth, rows, n), F32),
        compiler_params=pltpu.CompilerParams(
            dimension_semantics=("parallel", "parallel"),
            vmem_limit_bytes=_vmem_limit(4 * d * tn * 4)),
        name="ada_mod",
    )(cvec, ada_w, ada_b.reshape(depth, 1, n))


def _gla_proj_kernel(x_ref, shift_ref, scale_ref, gain_ref, w_ref, w2_ref, ba_ref, o_ref, *, groups, w_low,
                     o_dec):
    hn = _modulated_norm(x_ref[0], gain_ref[...], shift_ref[0], scale_ref[0]).astype(BF16)
    n_chunk = 2 * V7X_MXU_DIM

    def proj(n0, width):
        return jnp.dot(hn, w_ref[:, n0:n0 + width], preferred_element_type=F32)

    n_dec = w2_ref.shape[1]
    piece = V7X_MXU_DIM
    a_low = proj(w_low, w2_ref.shape[0]).astype(BF16)

    def log_decay_piece(head):
        c0 = head * piece
        z = jnp.dot(a_low, w2_ref[:, c0:c0 + piece], preferred_element_type=F32) + ba_ref[:, c0:c0 + piece]
        la = (jnp.minimum(z, 0.0) - jnp.log(1.0 + jnp.exp(-jnp.abs(z)))) * (1.0 / GLA_GATE_TAU)
        hi = la.astype(BF16)
        o_ref[0, head, :, o_dec:o_dec + piece] = hi
        o_ref[0, head, :, o_dec + piece:o_dec + 2 * piece] = (la - hi.astype(F32)).astype(BF16)

    pieces = list(range(n_dec // piece))
    for w0, o0, width, head0, gated in groups:
        y = proj(w0, n_chunk)
        y = (_silu(y) if gated else y).astype(o_ref.dtype)
        for i in range(n_chunk // width):
            o_ref[0, head0 + i, :, o0:o0 + width] = y[:, i * width:(i + 1) * width]
        if pieces:
            log_decay_piece(pieces.pop(0))
    for head in pieces:
        log_decay_piece(head)


def _gla_in_proj(x, mod, gain, w, w2, ba, *, kt, vt, tm, with_query):
    b, t, d = x.shape
    h = GLA_HEADS
    dk, dv = kt // h, vt // h
    n_chunk = 2 * V7X_MXU_DIM
    w_low = 2 * kt + 2 * vt
    fields = [(kt, kt, dk, False), (2 * kt, vt, dv, False)]
    if with_query:
        fields = [(0, kt, dk, False)] + fields + [(2 * kt + vt, vt, dv, True)]
    groups, o0 = [], 0
    for w0, width, per_head, gated in fields:
        assert width % n_chunk == 0 and n_chunk % per_head == 0
        groups += [(w0 + c, o0, per_head, c // per_head, gated) for c in range(0, width, n_chunk)]
        o0 += per_head
    groups.sort(key=lambda g: not g[4])
    o_dec, n_out = o0, o0 + 4 * dk
    per_batch = mod.shape[0] != 1
    mod_map = lambda idx: (lambda i, j: (i if per_batch else 0, 0, idx))
    assert t % tm == 0 and w.shape[1] == w_low + w2.shape[0] and w2.shape[1] == 2 * kt and 2 * dk == V7X_MXU_DIM
    return pl.pallas_call(
        functools.partial(_gla_proj_kernel, groups=tuple(groups), w_low=w_low, o_dec=o_dec),
        grid=(b, t // tm),
        in_specs=[
            pl.BlockSpec((1, tm, d), lambda i, j: (i, j, 0)),
            pl.BlockSpec((1, 1, d), mod_map(0)),
            pl.BlockSpec((1, 1, d), mod_map(1)),
            _resident((1, d)),
            _resident(w.shape),
            _resident(w2.shape),
            _resident(ba.shape),
        ],
        out_specs=pl.BlockSpec((1, h, tm, n_out), lambda i, j: (i, 0, j, 0)),
        out_shape=jax.ShapeDtypeStruct((b, h, t, n_out), BF16),
        compiler_params=pltpu.CompilerParams(
            dimension_semantics=("parallel", "parallel"),
            vmem_limit_bytes=_vmem_limit(48 * 1024 * 1024)),
        name="gla_in_proj",
    )(x, mod, mod, gain.reshape(1, d), w, w2, ba)


def _gla_kernel(p_ref, pc_ref, hg_ref, *refs, n_riders, n_chunks, n_ctx_chunks, dk, dv, q_scale):
    rider_in, (o_ref, *rider_out) = refs[:n_riders], refs[n_riders:2 * n_riders + 1]
    qs_scr, qe_scr, kst_scr, pt_scr, et_scr, ptc_scr, etc_scr, sst_scr = refs[2 * n_riders + 1:]
    for src, dst in zip(rider_in, rider_out):
        dst[...] = src[...].astype(dst.dtype)
    c_len = GLA_CHUNK
    mid = c_len // 2

    def columns(ref, widths):
        starts = [sum(widths[:i]) for i in range(len(widths))]
        return [lambda rows, c0=c0, w=w: ref[0, 0, rows, c0:c0 + w] for c0, w in zip(starts, widths)]

    q_at, k_at, v_at, gs_at, hi_at, lo_at = columns(p_ref, [dk, dk, dv, dv, 2 * dk, 2 * dk])
    kc_at, vc_at, hic_at, loc_at = columns(pc_ref, [dk, dv, 2 * dk, 2 * dk])

    row = lax.broadcasted_iota(jnp.int32, (c_len, c_len), 0)
    col = lax.broadcasted_iota(jnp.int32, (c_len, c_len), 1)
    lower = row >= col
    upper = row <= col
    tril = jnp.where(lower, 1.0, 0.0).astype(BF16)
    triu = jnp.where(upper, 1.0, 0.0).astype(BF16)

    def rows_of(i):
        return slice(i * c_len, (i + 1) * c_len)

    def prepare_stages(k_at, v_at, hi_at, lo_at, et_dst, pt_dst, with_query):
        def scaled_operands(i, _):
            rows = rows_of(i)
            hi, lo = hi_at(rows), lo_at(rows)
            pf = jnp.dot(tril, jnp.concatenate([hi[:, :dk], lo[:, :dk]], axis=1), preferred_element_type=F32)
            pb = jnp.dot(triu, jnp.concatenate([hi[:, dk:], lo[:, dk:]], axis=1), preferred_element_type=F32)
            bf = pf[:, :dk] + pf[:, dk:]
            bb = pb[:, :dk] + pb[:, dk:]
            rf, tf = bf[mid - 1:mid, :], bf[c_len - 1:c_len, :]
            rb, tb = bb[mid:mid + 1, :], bb[0:1, :]
            k = k_at(rows).astype(F32)
            dec_f = jnp.exp(bf - rf)
            dec_b = jnp.exp(bb - rb)
            ks_f = k * (1.0 / dec_f)
            ks_b = k * (1.0 / dec_b)
            et_dst[i] = jnp.concatenate([jnp.exp(tf), jnp.exp(tb)], axis=1)
            if with_query:
                q = q_at(rows).astype(F32) * q_scale
                qs_f = q * dec_f
                qs_b = q * dec_b
                qs_scr[rows, :] = jnp.concatenate([qs_f, qs_b], axis=1).astype(BF16)
                qe_scr[rows, :] = jnp.concatenate([qs_f * jnp.exp(rf), qs_b * jnp.exp(rb)], axis=1).astype(BF16)
                kst_scr[:dk, rows] = ks_f.T.astype(BF16)
                kst_scr[dk:, rows] = ks_b.T.astype(BF16)
            return jnp.concatenate([ks_f * jnp.exp(tf - rf), ks_b * jnp.exp(tb - rb)], axis=1).astype(BF16)

        def state_increment(i, kd):
            pt_dst[i] = lax.dot_general(v_at(rows_of(i)), kd, TN_DIMS, preferred_element_type=F32)

        return [scaled_operands, state_increment]

    ctx_stages = prepare_stages(kc_at, vc_at, hic_at, loc_at, etc_scr, ptc_scr, False)
    lat_stages = prepare_stages(k_at, v_at, hi_at, lo_at, et_scr, pt_scr, True)
    both = [lambda i, arg, c=c, l=l: c(i, arg) if i < n_ctx_chunks else l(i - n_ctx_chunks, arg)
            for c, l in zip(ctx_stages, lat_stages)]
    _run_staggered(both, n_ctx_chunks + n_chunks, lag=GLA_STAGE_LAG)

    s_f = jnp.zeros((dv, dk), F32)
    s_b = jnp.zeros((dv, dk), F32)
    for c in range(n_ctx_chunks):
        s_f = s_f * etc_scr[c, :, :dk] + ptc_scr[c, :, :dk]
    for c in reversed(range(n_ctx_chunks)):
        s_b = s_b * etc_scr[c, :, dk:] + ptc_scr[c, :, dk:]
    for c in range(n_chunks):
        sst_scr[c, :dk, :] = s_f.T.astype(BF16)
        s_f = s_f * et_scr[c, :, :dk] + pt_scr[c, :, :dk]
    for c in reversed(range(n_chunks)):
        sst_scr[c, dk:, :] = s_b.T.astype(BF16)
        s_b = s_b * et_scr[c, :, dk:] + pt_scr[c, :, dk:]

    def intra_scores(i, _):
        qs = qs_scr[rows_of(i), :]
        att_f = jnp.dot(qs[:, :dk], kst_scr[:dk, rows_of(i)], preferred_element_type=F32)
        att_b = jnp.dot(qs[:, dk:], kst_scr[dk:, rows_of(i)], preferred_element_type=F32)
        return (jnp.where(lower, att_f, 0.0) + jnp.where(upper, att_b, 0.0)).astype(BF16)

    def outputs(i, att):
        rows = rows_of(i)
        o = jnp.dot(att, v_at(rows), preferred_element_type=F32)
        o = o + jnp.dot(qe_scr[rows, :], sst_scr[i], preferred_element_type=F32)
        return o, jnp.mean(o * o, axis=-1, keepdims=True)

    def normalise(i, o_ms):
        rows = rows_of(i)
        on = o_ms[0] * lax.rsqrt(o_ms[1] + EPS)
        o_ref[0, rows, :] = ((on * hg_ref[...]) * gs_at(rows).astype(F32)).astype(o_ref.dtype)

    _run_staggered([intra_scores, outputs, normalise], n_chunks, lag=GLA_STAGE_LAG)


def _gla_scan(proj, proj_ctx, head_gain, riders, *, d_model):
    b, h, t, _ = proj.shape
    tc = proj_ctx.shape[2]
    kt, vt = d_model // 2, d_model
    dk, dv = kt // h, vt // h
    assert h == GLA_HEADS and t % GLA_CHUNK == 0 and tc % GLA_CHUNK == 0
    assert dk == V7X_LANES and dv == V7X_MXU_DIM and 2 * dk == V7X_MXU_DIM
    n_chunks, n_ctx_chunks = t // GLA_CHUNK, tc // GLA_CHUNK
    head_block = lambda a: pl.BlockSpec((1, 1) + a.shape[2:], lambda i, j: (i, j, 0, 0))
    flat = [r.reshape(-1, r.shape[-1]) for r in riders]
    slabs = [f.shape[0] // (b * h) for f in flat]
    bf16_rows = 16
    assert all(f.shape[0] == n * b * h and n % bf16_rows == 0 for f, n in zip(flat, slabs))
    rider_specs = [pl.BlockSpec((n, f.shape[1]), lambda i, j: (i * h + j, 0)) for f, n in zip(flat, slabs)]
    in_specs = [head_block(proj), head_block(proj_ctx), pl.BlockSpec((1, dv), lambda i, j: (0, 0))] + rider_specs
    tok = lambda width: pltpu.VMEM((t, width), BF16)
    increments = lambda n: [pltpu.VMEM((n, dv, 2 * dk), F32), pltpu.VMEM((n, 1, 2 * dk), F32)]
    out, *cast = pl.pallas_call(
        functools.partial(_gla_kernel, n_riders=len(flat), n_chunks=n_chunks, n_ctx_chunks=n_ctx_chunks, dk=dk,
                          dv=dv, q_scale=float(dk) ** -0.5),
        grid=(b, h),
        in_specs=in_specs,
        out_specs=[pl.BlockSpec((1, t, dv), lambda i, j: (i, 0, j))] + rider_specs,
        out_shape=[jax.ShapeDtypeStruct((b, t, vt), BF16)] + [jax.ShapeDtypeStruct(f.shape, BF16) for f in flat],
        scratch_shapes=[tok(2 * dk), tok(2 * dk), pltpu.VMEM((2 * dk, t), BF16)] + increments(n_chunks)
                       + increments(n_ctx_chunks) + [pltpu.VMEM((n_chunks, 2 * dk, dv), BF16)],
        compiler_params=pltpu.CompilerParams(
            dimension_semantics=("parallel", "parallel"),
            vmem_limit_bytes=_vmem_limit(48 * 1024 * 1024)),
        name="gla_scan",
    )(proj, proj_ctx, head_gain.reshape(1, dv), *flat)
    return out, [c.reshape(r.shape) for c, r in zip(cast, riders)]


def _rolling_steps(n_tiles, head, body):
    s = pl.program_id(0)
    pl.when(s == 0)(head)
    pl.when(jnp.logical_and(s > 0, s < n_tiles))(functools.partial(body, True))
    pl.when(s == n_tiles)(functools.partial(body, False))


def _rolling_slots():
    head_slot = pl.program_id(0) % 2
    return head_slot, 1 - head_slot


def _head_tile(s, n_tiles):
    return jnp.minimum(s, n_tiles - 1)


def _body_tile(s):
    return jnp.maximum(s - 1, 0)


def _shift_rows(u, k):
    z = jnp.zeros((abs(k), u.shape[1]), u.dtype)
    if k > 0:
        return jnp.concatenate([z, u[:-k]], axis=0)
    return jnp.concatenate([u[-k:], z], axis=0)


def _ffn_kernel(x_ref, pre_ref, mix_gate_ref, shift_ref, scale_ref, gate_ref, gain_ref, wmix_ref,
                wup_ref, cw_ref, cb_ref, wdn_ref, fin_ref, o_ref, x_scr, hn_scr, *, hidden, final_norm,
                n_tiles):
    rows_img, cols, d = x_ref.shape[1:]
    tm = rows_img * cols
    head_slot, body_slot = _rolling_slots()

    def head_matmul():
        return jnp.dot(pre_ref[0].reshape(tm, pre_ref.shape[3]), wmix_ref[0], preferred_element_type=F32)

    def head_finish(y):
        x = x_ref[0].reshape(tm, d) + mix_gate_ref[0] * y
        x_scr[head_slot] = x
        hn_scr[head_slot] = _modulated_norm(x, gain_ref[...], shift_ref[0], scale_ref[0]).astype(BF16)

    def conv(u, c0):
        w = cw_ref[:, c0:c0 + HID_CHUNK]
        return (_shift_rows(u, cols) * w[0:1] + u * w[1:2] + _shift_rows(u, -cols) * w[2:3]
                + cb_ref[:, c0:c0 + HID_CHUNK])

    def body(with_head):
        acts = []
        head_y = []

        def up_proj(j, _):
            c0 = j * HID_CHUNK
            hn = hn_scr[body_slot]
            u = (jnp.dot(hn, wup_ref[0, :, c0:c0 + HID_CHUNK], preferred_element_type=F32),
                 jnp.dot(hn, wup_ref[0, :, hidden + c0:hidden + c0 + HID_CHUNK], preferred_element_type=F32))
            if with_head and j == 1:
                head_y.append(head_matmul())
            return u

        def conv_gate(j, u):
            c0 = j * HID_CHUNK
            acts.append((conv(u[0], c0) * _silu(conv(u[1], hidden + c0))).astype(BF16))
            if with_head and j == 1:
                head_finish(head_y.pop())

        _run_staggered([up_proj, conv_gate], hidden // HID_CHUNK, newest_first=True)
        down = jnp.dot(jnp.concatenate(acts, axis=1), wdn_ref[0], preferred_element_type=F32)
        out = x_scr[body_slot] + gate_ref[0] * down
        if final_norm:
            out = out * lax.rsqrt(jnp.mean(out * out, axis=-1, keepdims=True) + EPS) * fin_ref[...]
        o_ref[0] = out.reshape(rows_img, cols, d)

    _rolling_steps(n_tiles, lambda: head_finish(head_matmul()), body)


def _mixer_out_conv_ffn(x, pre, mod, w_mix, gain, w_up, conv_w, conv_b, w_down, fin_gain, *, mix_layer, layer,
                        final_norm):
    b, t, d = x.shape
    hidden = w_down.shape[1]
    assert t % GRID_W == 0 and GRID_W % FFN_COLS == 0 and hidden % HID_CHUNK == 0 and hidden >= 2 * HID_CHUNK
    rows_img = t // GRID_W
    col_tiles = GRID_W // FFN_COLS
    n_tiles = b * col_tiles
    tm = rows_img * FFN_COLS
    x4 = x.reshape(b, rows_img, GRID_W, d)
    pre4 = pre.reshape(b, rows_img, GRID_W, pre.shape[2])
    head = lambda s: _head_tile(s, n_tiles)
    back = _body_tile
    head_tile = lambda width: pl.BlockSpec((1, rows_img, FFN_COLS, width),
                                           lambda s: (head(s) // col_tiles, 0, head(s) % col_tiles, 0))
    head_mod = lambda idx: pl.BlockSpec((1, 1, d), lambda s: (head(s) // col_tiles, 0, idx))
    const = _resident
    out = pl.pallas_call(
        functools.partial(_ffn_kernel, hidden=hidden, final_norm=final_norm, n_tiles=n_tiles),
        grid=(n_tiles + 1,),
        in_specs=[head_tile(d), head_tile(pre.shape[2]), head_mod(2), head_mod(3), head_mod(4),
                  pl.BlockSpec((1, 1, d), lambda s: (back(s) // col_tiles, 0, 5)),
                  const((1, d)), _resident_layer(w_mix.shape, mix_layer), _resident_layer(w_up.shape, layer),
                  const(conv_w.shape), const((1, 2 * hidden)), _resident_layer(w_down.shape, layer),
                  const((1, d))],
        out_specs=pl.BlockSpec((1, rows_img, FFN_COLS, d),
                               lambda s: (back(s) // col_tiles, 0, back(s) % col_tiles, 0)),
        out_shape=jax.ShapeDtypeStruct(x4.shape, F32),
        scratch_shapes=[pltpu.VMEM((2, tm, d), F32), pltpu.VMEM((2, tm, d), BF16)],
        compiler_params=pltpu.CompilerParams(
            dimension_semantics=("arbitrary",),
            vmem_limit_bytes=_vmem_limit(60 * 1024 * 1024)),
        name="conv_ffn",
    )(x4, pre4, mod, mod, mod, mod, gain.reshape(1, d), w_mix, w_up, conv_w, conv_b.reshape(1, 2 * hidden),
      w_down, fin_gain.reshape(1, d))
    return out.reshape(b, t, d)


def _short_conv_kernel(x_ref, shift_ref, scale_ref, gain_ref, win_ref, cw_ref, o_ref, *, width):
    x = x_ref[0]
    tm = x.shape[0]
    hn = _modulated_norm(x, gain_ref[...], shift_ref[0], scale_ref[0]).astype(BF16)
    pos = lax.broadcasted_iota(jnp.int32, (tm, SC_CHUNK), 0) % GRID_W
    has_left = pos != 0
    has_right = pos != GRID_W - 1
    for c0 in range(0, width, SC_CHUNK):
        cg, v, bg = (jnp.dot(hn, win_ref[0, :, base + c0:base + c0 + SC_CHUNK], preferred_element_type=F32)
                     for base in (width, 2 * width, 0))
        u = cg * v
        w = cw_ref[:, c0:c0 + SC_CHUNK]
        left = jnp.where(has_left, pltpu.roll(u, 1, axis=0), 0.0)
        right = jnp.where(has_right, pltpu.roll(u, tm - 1, axis=0), 0.0)
        o_ref[0, :, c0:c0 + SC_CHUNK] = (bg * (left * w[0:1] + u * w[1:2] + right * w[2:3])).astype(o_ref.dtype)


def _short_conv(x, mod, gain, w_in, conv_w, *, layer, tm):
    b, t, d = x.shape
    width = conv_w.shape[1]
    assert t % tm == 0 and tm % GRID_W == 0 and width % SC_CHUNK == 0
    mod_spec = lambda idx: pl.BlockSpec((1, 1, d), lambda i, j: (i, 0, idx))
    const = _resident
    tile = pl.BlockSpec((1, tm, d), lambda i, j: (i, j, 0))
    return pl.pallas_call(
        functools.partial(_short_conv_kernel, width=width),
        grid=(b, t // tm),
        in_specs=[tile, mod_spec(0), mod_spec(1), const((1, d)), _resident_layer(w_in.shape, layer),
                  const(conv_w.shape)],
        out_specs=pl.BlockSpec((1, tm, width), lambda i, j: (i, j, 0)),
        out_shape=jax.ShapeDtypeStruct((b, t, width), BF16),
        compiler_params=pltpu.CompilerParams(
            dimension_semantics=("parallel", "parallel"),
            vmem_limit_bytes=_vmem_limit(56 * 1024 * 1024)),
        name="short_conv",
    )(x, mod, mod, gain.reshape(1, d), w_in, conv_w)


def kernel(x, c, ctx, c_ctx, ada_w, ada_b, norm_mix, norm_ffn, gla_w_in, gla_w_a2, gla_b_a,
           gla_head_norm, gla_w_out, sc_w_in, sc_conv_w, sc_w_out, ffn_w_up, ffn_conv_w,
           ffn_conv_b, ffn_w_down, final_norm):
    b, t, d = x.shape
    depth = ada_w.shape[0]
    assert depth == 2 and gla_w_in.shape[0] == 1 and sc_w_in.shape[0] == 1
    h = GLA_HEADS
    kt, vt, rank = d // 2, d, GLA_GATE_RANK
    dk = kt // h

    pad = (-(b + 1)) % 8
    cvec = jnp.concatenate([c, c_ctx[None, :], jnp.zeros((pad, d), F32)], axis=0)
    mods = _ada_mod(cvec, ada_w, ada_b)
    mod_x = [mods[i, :b].reshape(b, 1, N_MOD * d) for i in range(depth)]
    mod_ctx = mods[0, b:b + 1].reshape(1, 1, N_MOD * d)

    n_in = gla_w_in.shape[2]
    n_pad = (-n_in) % V7X_MXU_DIM
    w_in = jnp.pad(gla_w_in[0], ((0, 0), (0, n_pad))).astype(BF16)
    a2 = gla_w_a2[0].reshape(2, rank, h, dk)
    w2 = jnp.zeros((h, V7X_MXU_DIM, 2 * dk), F32)
    w2 = w2.at[:, :rank, :dk].set(a2[0].transpose(1, 0, 2))
    w2 = w2.at[:, rank:2 * rank, dk:].set(a2[1].transpose(1, 0, 2)).astype(BF16)
    w2 = w2.transpose(1, 0, 2).reshape(V7X_MXU_DIM, 2 * kt)
    ba = gla_b_a[0].reshape(2, h, dk).transpose(1, 0, 2).reshape(1, 2 * kt)

    proj_ctx = _gla_in_proj(ctx, mod_ctx, norm_mix[0], w_in, w2, ba, kt=kt, vt=vt, tm=ctx.shape[1],
                            with_query=False)
    proj_x = _gla_in_proj(x, mod_x[0], norm_mix[0], w_in, w2, ba, kt=kt, vt=vt, tm=TOKEN_TILE, with_query=True)
    o, (w_up, w_down, w_mix0, w_sc_in, w_mix1) = _gla_scan(
        proj_x, proj_ctx, gla_head_norm[0], [ffn_w_up, ffn_w_down, gla_w_out, sc_w_in, sc_w_out], d_model=d)
    hcur = _mixer_out_conv_ffn(x, o, mod_x[0], w_mix0, norm_ffn[0], w_up, ffn_conv_w[0], ffn_conv_b[0], w_down,
                               final_norm, mix_layer=0, layer=0, final_norm=False)

    y = _short_conv(hcur, mod_x[1], norm_mix[1], w_sc_in, sc_conv_w[0], layer=0, tm=TOKEN_TILE)
    return _mixer_out_conv_ffn(hcur, y, mod_x[1], w_mix1, norm_ffn[1], w_up, ffn_conv_w[1], ffn_conv_b[1], w_down,
                               final_norm, mix_layer=0, layer=1, final_norm=True)
```

```python
import functools

import jax
import jax.numpy as jnp
from jax import lax
from jax.experimental import pallas as pl
from jax.experimental.pallas import tpu as pltpu

F32 = jnp.float32
BF16 = jnp.bfloat16

EPS = 1e-6
GRID_W = 64
N_MOD = 6
GLA_HEADS = 4
GLA_GATE_RANK = 16
GLA_GATE_TAU = 16.0
CONV_WIDTH = 3

V7X_LANES = 128
V7X_MXU_DIM = 256
V7X_VMEM_BYTES = 64 * 1024 * 1024

GLA_CHUNK = 128
GLA_PREPARE_LAG = 2
GLA_EMIT_LAG = 4
TOKEN_TILE = 1024
FFN_COLS = 16
HID_CHUNK = 512
SC_CHUNK = 256

TN_DIMS = (((0,), (0,)), ((), ()))


def _vmem_limit(nbytes):
    return int(min(nbytes, V7X_VMEM_BYTES - 4 * 1024 * 1024))


def _resident(shape):
    return pl.BlockSpec(shape, lambda *_: (0,) * len(shape), pipeline_mode=pl.Buffered(1))


def _resident_layer(stacked_shape, layer):
    return pl.BlockSpec((1,) + tuple(stacked_shape[1:]), lambda *_: (layer, 0, 0), pipeline_mode=pl.Buffered(1))


def _modulated_norm(x, gain, shift, scale):
    y = x * lax.rsqrt(jnp.mean(x * x, axis=-1, keepdims=True) + EPS)
    return y * (gain * (1.0 + scale)) + shift


def _silu(x):
    return x * jax.nn.sigmoid(x)


def _run_staggered(stages, n_items, newest_first=False, lag=1):
    handoff = [dict() for _ in stages]
    for t in range(n_items + (len(stages) - 1) * lag):
        order = range(len(stages)) if newest_first else reversed(range(len(stages)))
        for s in order:
            i = t - s * lag
            if 0 <= i < n_items:
                arg = handoff[s - 1].pop(i) if s else None
                handoff[s][i] = stages[s](i, arg)


def _ada_kernel(c_ref, w_ref, b_ref, o_ref):
    c = c_ref[...]
    sc = _silu(c).astype(BF16)
    o_ref[0] = jnp.dot(sc, w_ref[0].astype(BF16), preferred_element_type=F32) + b_ref[0]


def _ada_mod(cvec, ada_w, ada_b):
    depth, d, n = ada_w.shape
    rows = cvec.shape[0]
    tn = n // 4
    return pl.pallas_call(
        _ada_kernel,
        grid=(depth, n // tn),
        in_specs=[
            pl.BlockSpec((rows, d), lambda i, j: (0, 0)),
            pl.BlockSpec((1, d, tn), lambda i, j: (i, 0, j)),
            pl.BlockSpec((1, 1, tn), lambda i, j: (i, 0, j)),
        ],
        out_specs=pl.BlockSpec((1, rows, tn), lambda i, j: (i, 0, j)),
        out_shape=jax.ShapeDtypeStruct((depth, rows, n), F32),
        compiler_params=pltpu.CompilerParams(
            dimension_semantics=("parallel", "parallel"),
            vmem_limit_bytes=_vmem_limit(4 * d * tn * 4)),
        name="ada_mod",
    )(cvec, ada_w, ada_b.reshape(depth, 1, n))


def _gla_proj_kernel(x_ref, shift_ref, scale_ref, gain_ref, w_ref, w2_ref, ba_ref, o_ref, *, groups, w_low,
                     o_dec):
    hn = _modulated_norm(x_ref[0], gain_ref[...], shift_ref[0], scale_ref[0]).astype(BF16)
    n_chunk = 2 * V7X_MXU_DIM

    def proj(n0, width):
        return jnp.dot(hn, w_ref[:, n0:n0 + width], preferred_element_type=F32)

    n_dec = w2_ref.shape[1]
    piece = V7X_MXU_DIM
    a_low = proj(w_low, w2_ref.shape[0]).astype(BF16)

    def log_decay_piece(head):
        c0 = head * piece
        z = jnp.dot(a_low, w2_ref[:, c0:c0 + piece], preferred_element_type=F32) + ba_ref[:, c0:c0 + piece]
        la = (jnp.minimum(z, 0.0) - jnp.log(1.0 + jnp.exp(-jnp.abs(z)))) * (1.0 / GLA_GATE_TAU)
        hi = la.astype(BF16)
        o_ref[0, head, :, o_dec:o_dec + piece] = hi
        o_ref[0, head, :, o_dec + piece:o_dec + 2 * piece] = (la - hi.astype(F32)).astype(BF16)

    pieces = list(range(n_dec // piece))
    for w0, o0, width, head0, gated in groups:
        y = proj(w0, n_chunk)
        y = (_silu(y) if gated else y).astype(o_ref.dtype)
        for i in range(n_chunk // width):
            o_ref[0, head0 + i, :, o0:o0 + width] = y[:, i * width:(i + 1) * width]
        if pieces:
            log_decay_piece(pieces.pop(0))
    for head in pieces:
        log_decay_piece(head)


def _gla_in_proj(x, mod, gain, w, w2, ba, *, kt, vt, tm, with_query):
    b, t, d = x.shape
    h = GLA_HEADS
    dk, dv = kt // h, vt // h
    n_chunk = 2 * V7X_MXU_DIM
    w_low = 2 * kt + 2 * vt
    fields = [(kt, kt, dk, False), (2 * kt, vt, dv, False)]
    if with_query:
        fields = [(0, kt, dk, False)] + fields + [(2 * kt + vt, vt, dv, True)]
    groups, o0 = [], 0
    for w0, width, per_head, gated in fields:
        assert width % n_chunk == 0 and n_chunk % per_head == 0
        groups += [(w0 + c, o0, per_head, c // per_head, gated) for c in range(0, width, n_chunk)]
        o0 += per_head
    groups.sort(key=lambda g: not g[4])
    o_dec, n_out = o0, o0 + 4 * dk
    per_batch = mod.shape[0] != 1
    mod_map = lambda idx: (lambda i, j: (i if per_batch else 0, 0, idx))
    assert t % tm == 0 and w.shape[1] == w_low + w2.shape[0] and w2.shape[1] == 2 * kt and 2 * dk == V7X_MXU_DIM
    return pl.pallas_call(
        functools.partial(_gla_proj_kernel, groups=tuple(groups), w_low=w_low, o_dec=o_dec),
        grid=(b, t // tm),
        in_specs=[
            pl.BlockSpec((1, tm, d), lambda i, j: (i, j, 0)),
            pl.BlockSpec((1, 1, d), mod_map(0)),
            pl.BlockSpec((1, 1, d), mod_map(1)),
            _resident((1, d)),
            _resident(w.shape),
            _resident(w2.shape),
            _resident(ba.shape),
        ],
        out_specs=pl.BlockSpec((1, h, tm, n_out), lambda i, j: (i, 0, j, 0)),
        out_shape=jax.ShapeDtypeStruct((b, h, t, n_out), BF16),
        compiler_params=pltpu.CompilerParams(
            dimension_semantics=("parallel", "parallel"),
            vmem_limit_bytes=_vmem_limit(48 * 1024 * 1024)),
        name="gla_in_proj",
    )(x, mod, mod, gain.reshape(1, d), w, w2, ba)


def _gla_kernel(p_ref, pc_ref, hg_ref, *refs, n_riders, n_chunks, n_ctx_chunks, dk, dv, q_scale):
    rider_in, (o_ref, *rider_out) = refs[:n_riders], refs[n_riders:2 * n_riders + 1]
    qs_scr, qe_scr, kst_scr, pt_scr, et_scr, ptc_scr, etc_scr, sst_scr = refs[2 * n_riders + 1:]
    for src, dst in zip(rider_in, rider_out):
        dst[...] = src[...].astype(dst.dtype)
    c_len = GLA_CHUNK
    mid = c_len // 2

    def columns(ref, widths):
        starts = [sum(widths[:i]) for i in range(len(widths))]
        return [lambda rows, c0=c0, w=w: ref[0, 0, rows, c0:c0 + w] for c0, w in zip(starts, widths)]

    q_at, k_at, v_at, gs_at, hi_at, lo_at = columns(p_ref, [dk, dk, dv, dv, 2 * dk, 2 * dk])
    kc_at, vc_at, hic_at, loc_at = columns(pc_ref, [dk, dv, 2 * dk, 2 * dk])

    row = lax.broadcasted_iota(jnp.int32, (c_len, c_len), 0)
    col = lax.broadcasted_iota(jnp.int32, (c_len, c_len), 1)
    lower = row >= col
    upper = row <= col
    tril = jnp.where(lower, 1.0, 0.0).astype(BF16)
    triu = jnp.where(upper, 1.0, 0.0).astype(BF16)

    def rows_of(i):
        return slice(i * c_len, (i + 1) * c_len)

    def prepare_stages(k_at, v_at, hi_at, lo_at, et_dst, pt_dst, with_query):
        def scaled_operands(i, _):
            rows = rows_of(i)
            hi, lo = hi_at(rows), lo_at(rows)
            pf = jnp.dot(tril, jnp.concatenate([hi[:, :dk], lo[:, :dk]], axis=1), preferred_element_type=F32)
            pb = jnp.dot(triu, jnp.concatenate([hi[:, dk:], lo[:, dk:]], axis=1), preferred_element_type=F32)
            bf = pf[:, :dk] + pf[:, dk:]
            bb = pb[:, :dk] + pb[:, dk:]
            rf, tf = bf[mid - 1:mid, :], bf[c_len - 1:c_len, :]
            rb, tb = bb[mid:mid + 1, :], bb[0:1, :]
            k = k_at(rows).astype(F32)
            dec_f = jnp.exp(bf - rf)
            dec_b = jnp.exp(bb - rb)
            ks_f = k * (1.0 / dec_f)
            ks_b = k * (1.0 / dec_b)
            et_dst[i] = jnp.concatenate([jnp.exp(tf), jnp.exp(tb)], axis=1)
            if with_query:
                q = q_at(rows).astype(F32) * q_scale
                qs_f = q * dec_f
                qs_b = q * dec_b
                qs_scr[rows, :] = jnp.concatenate([qs_f, qs_b], axis=1).astype(BF16)
                qe_scr[rows, :] = jnp.concatenate([qs_f * jnp.exp(rf), qs_b * jnp.exp(rb)], axis=1).astype(BF16)
                kst_scr[:dk, rows] = ks_f.T.astype(BF16)
                kst_scr[dk:, rows] = ks_b.T.astype(BF16)
            return jnp.concatenate([ks_f * jnp.exp(tf - rf), ks_b * jnp.exp(tb - rb)], axis=1).astype(BF16)

        def state_increment(i, kd):
            pt_dst[i] = lax.dot_general(v_at(rows_of(i)), kd, TN_DIMS, preferred_element_type=F32)

        return [scaled_operands, state_increment]

    ctx_stages = prepare_stages(kc_at, vc_at, hic_at, loc_at, etc_scr, ptc_scr, False)
    lat_stages = prepare_stages(k_at, v_at, hi_at, lo_at, et_scr, pt_scr, True)
    both = [lambda i, arg, c=c, l=l: c(i, arg) if i < n_ctx_chunks else l(i - n_ctx_chunks, arg)
            for c, l in zip(ctx_stages, lat_stages)]
    _run_staggered(both, n_ctx_chunks + n_chunks, lag=GLA_PREPARE_LAG)

    s_f = jnp.zeros((dv, dk), F32)
    s_b = jnp.zeros((dv, dk), F32)
    for c in range(n_ctx_chunks):
        s_f = s_f * etc_scr[c, :, :dk] + ptc_scr[c, :, :dk]
    for c in reversed(range(n_ctx_chunks)):
        s_b = s_b * etc_scr[c, :, dk:] + ptc_scr[c, :, dk:]
    for c in range(n_chunks):
        sst_scr[c, :dk, :] = s_f.T.astype(BF16)
        s_f = s_f * et_scr[c, :, :dk] + pt_scr[c, :, :dk]
    for c in reversed(range(n_chunks)):
        sst_scr[c, dk:, :] = s_b.T.astype(BF16)
        s_b = s_b * et_scr[c, :, dk:] + pt_scr[c, :, dk:]

    def intra_scores(i, _):
        qs = qs_scr[rows_of(i), :]
        att_f = jnp.dot(qs[:, :dk], kst_scr[:dk, rows_of(i)], preferred_element_type=F32)
        att_b = jnp.dot(qs[:, dk:], kst_scr[dk:, rows_of(i)], preferred_element_type=F32)
        return (jnp.where(lower, att_f, 0.0) + jnp.where(upper, att_b, 0.0)).astype(BF16)

    def outputs(i, att):
        rows = rows_of(i)
        o = jnp.dot(att, v_at(rows), preferred_element_type=F32)
        o = o + jnp.dot(qe_scr[rows, :], sst_scr[i], preferred_element_type=F32)
        return o, jnp.mean(o * o, axis=-1, keepdims=True)

    def normalise(i, o_ms):
        rows = rows_of(i)
        on = o_ms[0] * lax.rsqrt(o_ms[1] + EPS)
        o_ref[0, rows, :] = ((on * hg_ref[...]) * gs_at(rows).astype(F32)).astype(o_ref.dtype)

    _run_staggered([intra_scores, outputs, normalise], n_chunks, lag=GLA_EMIT_LAG)


def _gla_scan(proj, proj_ctx, head_gain, riders, *, d_model):
    b, h, t, _ = proj.shape
    tc = proj_ctx.shape[2]
    kt, vt = d_model // 2, d_model
    dk, dv = kt // h, vt // h
    assert h == GLA_HEADS and t % GLA_CHUNK == 0 and tc % GLA_CHUNK == 0
    assert dk == V7X_LANES and dv == V7X_MXU_DIM and 2 * dk == V7X_MXU_DIM
    n_chunks, n_ctx_chunks = t // GLA_CHUNK, tc // GLA_CHUNK
    head_block = lambda a: pl.BlockSpec((1, 1) + a.shape[2:], lambda i, j: (i, j, 0, 0))
    flat = [r.reshape(-1, r.shape[-1]) for r in riders]
    slabs = [f.shape[0] // (b * h) for f in flat]
    bf16_rows = 16
    assert all(f.shape[0] == n * b * h and n % bf16_rows == 0 for f, n in zip(flat, slabs))
    rider_specs = [pl.BlockSpec((n, f.shape[1]), lambda i, j: (i * h + j, 0)) for f, n in zip(flat, slabs)]
    in_specs = [head_block(proj), head_block(proj_ctx), pl.BlockSpec((1, dv), lambda i, j: (0, 0))] + rider_specs
    tok = lambda width: pltpu.VMEM((t, width), BF16)
    increments = lambda n: [pltpu.VMEM((n, dv, 2 * dk), F32), pltpu.VMEM((n, 1, 2 * dk), F32)]
    out, *cast = pl.pallas_call(
        functools.partial(_gla_kernel, n_riders=len(flat), n_chunks=n_chunks, n_ctx_chunks=n_ctx_chunks, dk=dk,
                          dv=dv, q_scale=float(dk) ** -0.5),
        grid=(b, h),
        in_specs=in_specs,
        out_specs=[pl.BlockSpec((1, t, dv), lambda i, j: (i, 0, j))] + rider_specs,
        out_shape=[jax.ShapeDtypeStruct((b, t, vt), BF16)] + [jax.ShapeDtypeStruct(f.shape, BF16) for f in flat],
        scratch_shapes=[tok(2 * dk), tok(2 * dk), pltpu.VMEM((2 * dk, t), BF16)] + increments(n_chunks)
                       + increments(n_ctx_chunks) + [pltpu.VMEM((n_chunks, 2 * dk, dv), BF16)],
        compiler_params=pltpu.CompilerParams(
            dimension_semantics=("parallel", "parallel"),
            vmem_limit_bytes=_vmem_limit(48 * 1024 * 1024)),
        name="gla_scan",
    )(proj, proj_ctx, head_gain.reshape(1, dv), *flat)
    return out, [c.reshape(r.shape) for c, r in zip(cast, riders)]


def _rolling_steps(n_tiles, head, body):
    s = pl.program_id(0)
    pl.when(s == 0)(head)
    pl.when(jnp.logical_and(s > 0, s < n_tiles))(functools.partial(body, True))
    pl.when(s == n_tiles)(functools.partial(body, False))


def _rolling_slots():
    head_slot = pl.program_id(0) % 2
    return head_slot, 1 - head_slot


def _head_tile(s, n_tiles):
    return jnp.minimum(s, n_tiles - 1)


def _body_tile(s):
    return jnp.maximum(s - 1, 0)


def _shift_rows(u, k):
    z = jnp.zeros((abs(k), u.shape[1]), u.dtype)
    if k > 0:
        return jnp.concatenate([z, u[:-k]], axis=0)
    return jnp.concatenate([u[-k:], z], axis=0)


def _ffn_kernel(x_ref, pre_ref, mix_gate_ref, shift_ref, scale_ref, gate_ref, gain_ref, wmix_ref,
                wup_ref, cw_ref, cb_ref, wdn_ref, fin_ref, o_ref, x_scr, hn_scr, *, hidden, final_norm,
                n_tiles):
    rows_img, cols, d = x_ref.shape[1:]
    tm = rows_img * cols
    head_slot, body_slot = _rolling_slots()

    def head_matmul():
        return jnp.dot(pre_ref[0].reshape(tm, pre_ref.shape[3]), wmix_ref[0], preferred_element_type=F32)

    def head_finish(y):
        x = x_ref[0].reshape(tm, d) + mix_gate_ref[0] * y
        x_scr[head_slot] = x
        hn_scr[head_slot] = _modulated_norm(x, gain_ref[...], shift_ref[0], scale_ref[0]).astype(BF16)

    def conv(u, c0):
        w = cw_ref[:, c0:c0 + HID_CHUNK]
        return (_shift_rows(u, cols) * w[0:1] + u * w[1:2] + _shift_rows(u, -cols) * w[2:3]
                + cb_ref[:, c0:c0 + HID_CHUNK])

    def body(with_head):
        acts = []
        head_y = []

        def up_proj(j, _):
            c0 = j * HID_CHUNK
            hn = hn_scr[body_slot]
            u = (jnp.dot(hn, wup_ref[0, :, c0:c0 + HID_CHUNK], preferred_element_type=F32),
                 jnp.dot(hn, wup_ref[0, :, hidden + c0:hidden + c0 + HID_CHUNK], preferred_element_type=F32))
            if with_head and j == 1:
                head_y.append(head_matmul())
            return u

        def conv_gate(j, u):
            c0 = j * HID_CHUNK
            acts.append((conv(u[0], c0) * _silu(conv(u[1], hidden + c0))).astype(BF16))
            if with_head and j == 1:
                head_finish(head_y.pop())

        _run_staggered([up_proj, conv_gate], hidden // HID_CHUNK, newest_first=True)
        down = jnp.dot(jnp.concatenate(acts, axis=1), wdn_ref[0], preferred_element_type=F32)
        out = x_scr[body_slot] + gate_ref[0] * down
        if final_norm:
            out = out * lax.rsqrt(jnp.mean(out * out, axis=-1, keepdims=True) + EPS) * fin_ref[...]
        o_ref[0] = out.reshape(rows_img, cols, d)

    _rolling_steps(n_tiles, lambda: head_finish(head_matmul()), body)


def _mixer_out_conv_ffn(x, pre, mod, w_mix, gain, w_up, conv_w, conv_b, w_down, fin_gain, *, mix_layer, layer,
                        final_norm):
    b, t, d = x.shape
    hidden = w_down.shape[1]
    assert t % GRID_W == 0 and GRID_W % FFN_COLS == 0 and hidden % HID_CHUNK == 0 and hidden >= 2 * HID_CHUNK
    rows_img = t // GRID_W
    col_tiles = GRID_W // FFN_COLS
    n_tiles = b * col_tiles
    tm = rows_img * FFN_COLS
    x4 = x.reshape(b, rows_img, GRID_W, d)
    pre4 = pre.reshape(b, rows_img, GRID_W, pre.shape[2])
    head = lambda s: _head_tile(s, n_tiles)
    back = _body_tile
    head_tile = lambda width: pl.BlockSpec((1, rows_img, FFN_COLS, width),
                                           lambda s: (head(s) // col_tiles, 0, head(s) % col_tiles, 0))
    head_mod = lambda idx: pl.BlockSpec((1, 1, d), lambda s: (head(s) // col_tiles, 0, idx))
    const = _resident
    out = pl.pallas_call(
        functools.partial(_ffn_kernel, hidden=hidden, final_norm=final_norm, n_tiles=n_tiles),
        grid=(n_tiles + 1,),
        in_specs=[head_tile(d), head_tile(pre.shape[2]), head_mod(2), head_mod(3), head_mod(4),
                  pl.BlockSpec((1, 1, d), lambda s: (back(s) // col_tiles, 0, 5)),
                  const((1, d)), _resident_layer(w_mix.shape, mix_layer), _resident_layer(w_up.shape, layer),
                  const(conv_w.shape), const((1, 2 * hidden)), _resident_layer(w_down.shape, layer),
                  const((1, d))],
        out_specs=pl.BlockSpec((1, rows_img, FFN_COLS, d),
                               lambda s: (back(s) // col_tiles, 0, back(s) % col_tiles, 0)),
        out_shape=jax.ShapeDtypeStruct(x4.shape, F32),
        scratch_shapes=[pltpu.VMEM((2, tm, d), F32), pltpu.VMEM((2, tm, d), BF16)],
        compiler_params=pltpu.CompilerParams(
            dimension_semantics=("arbitrary",),
            vmem_limit_bytes=_vmem_limit(60 * 1024 * 1024)),
        name="conv_ffn",
    )(x4, pre4, mod, mod, mod, mod, gain.reshape(1, d), w_mix, w_up, conv_w, conv_b.reshape(1, 2 * hidden),
      w_down, fin_gain.reshape(1, d))
    return out.reshape(b, t, d)


def _short_conv_kernel(x_ref, shift_ref, scale_ref, gain_ref, win_ref, cw_ref, o_ref, *, width):
    x = x_ref[0]
    tm = x.shape[0]
    hn = _modulated_norm(x, gain_ref[...], shift_ref[0], scale_ref[0]).astype(BF16)
    pos = lax.broadcasted_iota(jnp.int32, (tm, SC_CHUNK), 0) % GRID_W
    has_left = pos != 0
    has_right = pos != GRID_W - 1
    for c0 in range(0, width, SC_CHUNK):
        cg, v, bg = (jnp.dot(hn, win_ref[0, :, base + c0:base + c0 + SC_CHUNK], preferred_element_type=F32)
                     for base in (width, 2 * width, 0))
        u = cg * v
        w = cw_ref[:, c0:c0 + SC_CHUNK]
        left = jnp.where(has_left, pltpu.roll(u, 1, axis=0), 0.0)
        right = jnp.where(has_right, pltpu.roll(u, tm - 1, axis=0), 0.0)
        o_ref[0, :, c0:c0 + SC_CHUNK] = (bg * (left * w[0:1] + u * w[1:2] + right * w[2:3])).astype(o_ref.dtype)


def _short_conv(x, mod, gain, w_in, conv_w, *, layer, tm):
    b, t, d = x.shape
    width = conv_w.shape[1]
    assert t % tm == 0 and tm % GRID_W == 0 and width % SC_CHUNK == 0
    mod_spec = lambda idx: pl.BlockSpec((1, 1, d), lambda i, j: (i, 0, idx))
    const = _resident
    tile = pl.BlockSpec((1, tm, d), lambda i, j: (i, j, 0))
    return pl.pallas_call(
        functools.partial(_short_conv_kernel, width=width),
        grid=(b, t // tm),
        in_specs=[tile, mod_spec(0), mod_spec(1), const((1, d)), _resident_layer(w_in.shape, layer),
                  const(conv_w.shape)],
        out_specs=pl.BlockSpec((1, tm, width), lambda i, j: (i, j, 0)),
        out_shape=jax.ShapeDtypeStruct((b, t, width), BF16),
        compiler_params=pltpu.CompilerParams(
            dimension_semantics=("parallel", "parallel"),
            vmem_limit_bytes=_vmem_limit(56 * 1024 * 1024)),
        name="short_conv",
    )(x, mod, mod, gain.reshape(1, d), w_in, conv_w)


def kernel(x, c, ctx, c_ctx, ada_w, ada_b, norm_mix, norm_ffn, gla_w_in, gla_w_a2, gla_b_a,
           gla_head_norm, gla_w_out, sc_w_in, sc_conv_w, sc_w_out, ffn_w_up, ffn_conv_w,
           ffn_conv_b, ffn_w_down, final_norm):
    b, t, d = x.shape
    depth = ada_w.shape[0]
    assert depth == 2 and gla_w_in.shape[0] == 1 and sc_w_in.shape[0] == 1
    h = GLA_HEADS
    kt, vt, rank = d // 2, d, GLA_GATE_RANK
    dk = kt // h

    pad = (-(b + 1)) % 8
    cvec = jnp.concatenate([c, c_ctx[None, :], jnp.zeros((pad, d), F32)], axis=0)
    mods = _ada_mod(cvec, ada_w, ada_b)
    mod_x = [mods[i, :b].reshape(b, 1, N_MOD * d) for i in range(depth)]
    mod_ctx = mods[0, b:b + 1].reshape(1, 1, N_MOD * d)

    n_in = gla_w_in.shape[2]
    n_pad = (-n_in) % V7X_MXU_DIM
    w_in = jnp.pad(gla_w_in[0], ((0, 0), (0, n_pad))).astype(BF16)
    a2 = gla_w_a2[0].reshape(2, rank, h, dk)
    w2 = jnp.zeros((h, V7X_MXU_DIM, 2 * dk), F32)
    w2 = w2.at[:, :rank, :dk].set(a2[0].transpose(1, 0, 2))
    w2 = w2.at[:, rank:2 * rank, dk:].set(a2[1].transpose(1, 0, 2)).astype(BF16)
    w2 = w2.transpose(1, 0, 2).reshape(V7X_MXU_DIM, 2 * kt)
    ba = gla_b_a[0].reshape(2, h, dk).transpose(1, 0, 2).reshape(1, 2 * kt)

    proj_ctx = _gla_in_proj(ctx, mod_ctx, norm_mix[0], w_in, w2, ba, kt=kt, vt=vt, tm=ctx.shape[1],
                            with_query=False)
    proj_x = _gla_in_proj(x, mod_x[0], norm_mix[0], w_in, w2, ba, kt=kt, vt=vt, tm=TOKEN_TILE, with_query=True)
    o, (w_up, w_down, w_mix0, w_sc_in, w_mix1) = _gla_scan(
        proj_x, proj_ctx, gla_head_norm[0], [ffn_w_up, ffn_w_down, gla_w_out, sc_w_in, sc_w_out], d_model=d)
    hcur = _mixer_out_conv_ffn(x, o, mod_x[0], w_mix0, norm_ffn[0], w_up, ffn_conv_w[0], ffn_conv_b[0], w_down,
                               final_norm, mix_layer=0, layer=0, final_norm=False)

    y = _short_conv(hcur, mod_x[1], norm_mix[1], w_sc_in, sc_conv_w[0], layer=0, tm=TOKEN_TILE)
    return _mixer_out_conv_ffn(hcur, y, mod_x[1], w_mix1, norm_ffn[1], w_up, ffn_conv_w[1], ffn_conv_b[1], w_down,
                               final_norm, mix_layer=0, layer=1, final_norm=True)
```

```python
import functools
import math

import jax
import jax.numpy as jnp
from jax import lax
from jax.experimental import pallas as pl
from jax.experimental.pallas import tpu as pltpu

F32 = jnp.float32
BF16 = jnp.bfloat16

EPS = 1e-6
GRID_W = 64
N_MOD = 6
GLA_HEADS = 4
GLA_GATE_RANK = 16
GLA_GATE_TAU = 16.0
CONV_WIDTH = 3

V7X_LANES = 128
V7X_MXU_DIM = 256
V7X_VMEM_BYTES = 64 * 1024 * 1024

GLA_CHUNK = 128
GLA_PREPARE_LAG = 2
GLA_EMIT_LAG = 4
TOKEN_TILE = 1024
FFN_COLS = 16
HID_CHUNK = 512
SC_CHUNK = 256

TN_DIMS = (((0,), (0,)), ((), ()))


def _vmem_limit(nbytes):
    return int(min(nbytes, V7X_VMEM_BYTES - 4 * 1024 * 1024))


def _resident(shape):
    return pl.BlockSpec(shape, lambda *_: (0,) * len(shape), pipeline_mode=pl.Buffered(1))


def _resident_layer(stacked_shape, layer):
    return pl.BlockSpec((1,) + tuple(stacked_shape[1:]), lambda *_: (layer, 0, 0), pipeline_mode=pl.Buffered(1))


def _modulated_norm(x, gain, shift, scale):
    y = x * lax.rsqrt(jnp.mean(x * x, axis=-1, keepdims=True) + EPS)
    return y * (gain * (1.0 + scale)) + shift


def _silu(x):
    return x * jax.nn.sigmoid(x)


def _run_staggered(stages, n_items, newest_first=False, lag=1):
    handoff = [dict() for _ in stages]
    for t in range(n_items + (len(stages) - 1) * lag):
        order = range(len(stages)) if newest_first else reversed(range(len(stages)))
        for s in order:
            i = t - s * lag
            if 0 <= i < n_items:
                arg = handoff[s - 1].pop(i) if s else None
                handoff[s][i] = stages[s](i, arg)


def _ada_kernel(c_ref, w_ref, b_ref, o_ref):
    c = c_ref[...]
    sc = _silu(c).astype(BF16)
    o_ref[0] = jnp.dot(sc, w_ref[0].astype(BF16), preferred_element_type=F32) + b_ref[0]


def _ada_mod(cvec, ada_w, ada_b):
    depth, d, n = ada_w.shape
    rows = cvec.shape[0]
    tn = n // 4
    return pl.pallas_call(
        _ada_kernel,
        grid=(depth, n // tn),
        in_specs=[
            pl.BlockSpec((rows, d), lambda i, j: (0, 0)),
            pl.BlockSpec((1, d, tn), lambda i, j: (i, 0, j)),
            pl.BlockSpec((1, 1, tn), lambda i, j: (i, 0, j)),
        ],
        out_specs=pl.BlockSpec((1, rows, tn), lambda i, j: (i, 0, j)),
        out_shape=jax.ShapeDtypeStruct((depth, rows, n), F32),
        compiler_params=pltpu.CompilerParams(
            dimension_semantics=("parallel", "parallel"),
            vmem_limit_bytes=_vmem_limit(4 * d * tn * 4)),
        name="ada_mod",
    )(cvec, ada_w, ada_b.reshape(depth, 1, n))


def _gla_proj_kernel(x_ref, shift_ref, scale_ref, gain_ref, w_ref, w2_ref, ba_ref, o_ref, *, groups, w_low,
                     o_dec):
    hn = _modulated_norm(x_ref[0], gain_ref[...], shift_ref[0], scale_ref[0]).astype(BF16)
    n_chunk = 2 * V7X_MXU_DIM

    def proj(n0, width):
        return jnp.dot(hn, w_ref[:, n0:n0 + width], preferred_element_type=F32)

    n_dec = w2_ref.shape[1]
    piece = V7X_MXU_DIM
    a_low = proj(w_low, w2_ref.shape[0]).astype(BF16)

    def log_decay_piece(head):
        c0 = head * piece
        z = jnp.dot(a_low, w2_ref[:, c0:c0 + piece], preferred_element_type=F32) + ba_ref[:, c0:c0 + piece]
        la = (jnp.minimum(z, 0.0) - jnp.log(1.0 + jnp.exp(-jnp.abs(z)))) * (1.0 / GLA_GATE_TAU)
        hi = la.astype(BF16)
        o_ref[0, head, :, o_dec:o_dec + piece] = hi
        o_ref[0, head, :, o_dec + piece:o_dec + 2 * piece] = (la - hi.astype(F32)).astype(BF16)

    pieces = list(range(n_dec // piece))
    for w0, o0, width, head0, gated in groups:
        y = proj(w0, n_chunk)
        y = (_silu(y) if gated else y).astype(o_ref.dtype)
        for i in range(n_chunk // width):
            o_ref[0, head0 + i, :, o0:o0 + width] = y[:, i * width:(i + 1) * width]
        if pieces:
            log_decay_piece(pieces.pop(0))
    for head in pieces:
        log_decay_piece(head)


def _gla_in_proj(x, mod, gain, w, w2, ba, *, kt, vt, tm, with_query):
    b, t, d = x.shape
    h = GLA_HEADS
    dk, dv = kt // h, vt // h
    n_chunk = 2 * V7X_MXU_DIM
    w_low = 2 * kt + 2 * vt
    fields = [(kt, kt, dk, False), (2 * kt, vt, dv, False)]
    if with_query:
        fields = [(0, kt, dk, False)] + fields + [(2 * kt + vt, vt, dv, True)]
    groups, o0 = [], 0
    for w0, width, per_head, gated in fields:
        assert width % n_chunk == 0 and n_chunk % per_head == 0
        groups += [(w0 + c, o0, per_head, c // per_head, gated) for c in range(0, width, n_chunk)]
        o0 += per_head
    groups.sort(key=lambda g: not g[4])
    o_dec, n_out = o0, o0 + 4 * dk
    per_batch = mod.shape[0] != 1
    mod_map = lambda idx: (lambda i, j: (i if per_batch else 0, 0, idx))
    assert t % tm == 0 and w.shape[1] == w_low + w2.shape[0] and w2.shape[1] == 2 * kt and 2 * dk == V7X_MXU_DIM
    return pl.pallas_call(
        functools.partial(_gla_proj_kernel, groups=tuple(groups), w_low=w_low, o_dec=o_dec),
        grid=(b, t // tm),
        in_specs=[
            pl.BlockSpec((1, tm, d), lambda i, j: (i, j, 0)),
            pl.BlockSpec((1, 1, d), mod_map(0)),
            pl.BlockSpec((1, 1, d), mod_map(1)),
            _resident((1, d)),
            _resident(w.shape),
            _resident(w2.shape),
            _resident(ba.shape),
        ],
        out_specs=pl.BlockSpec((1, h, tm, n_out), lambda i, j: (i, 0, j, 0)),
        out_shape=jax.ShapeDtypeStruct((b, h, t, n_out), BF16),
        compiler_params=pltpu.CompilerParams(
            dimension_semantics=("parallel", "parallel"),
            vmem_limit_bytes=_vmem_limit(48 * 1024 * 1024)),
        name="gla_in_proj",
    )(x, mod, mod, gain.reshape(1, d), w, w2, ba)


def _gla_kernel(p_ref, pc_ref, hg_ref, *refs, n_riders, n_chunks, n_ctx_chunks, dk, dv, q_scale):
    rider_in, (o_ref, *rider_out) = refs[:n_riders], refs[n_riders:2 * n_riders + 1]
    qs_scr, qe_scr, kst_scr, pt_scr, et_scr, ptc_scr, etc_scr, sst_scr = refs[2 * n_riders + 1:]
    for src, dst in zip(rider_in, rider_out):
        dst[...] = src[...].astype(dst.dtype)
    c_len = GLA_CHUNK
    mid = c_len // 2

    def columns(ref, widths):
        starts = [sum(widths[:i]) for i in range(len(widths))]
        return [lambda rows, c0=c0, w=w: ref[0, 0, rows, c0:c0 + w] for c0, w in zip(starts, widths)]

    q_at, k_at, v_at, gs_at, hi_at, lo_at = columns(p_ref, [dk, dk, dv, dv, 2 * dk, 2 * dk])
    kc_at, vc_at, hic_at, loc_at = columns(pc_ref, [dk, dv, 2 * dk, 2 * dk])

    row = lax.broadcasted_iota(jnp.int32, (c_len, c_len), 0)
    col = lax.broadcasted_iota(jnp.int32, (c_len, c_len), 1)
    lower = row >= col
    upper = row <= col
    tril = jnp.where(lower, 1.0, 0.0).astype(BF16)
    triu = jnp.where(upper, 1.0, 0.0).astype(BF16)

    def rows_of(i):
        return slice(i * c_len, (i + 1) * c_len)

    def prepare_stages(k_at, v_at, hi_at, lo_at, et_dst, pt_dst, with_query):
        def scaled_operands(i, _):
            rows = rows_of(i)
            hi, lo = hi_at(rows), lo_at(rows)
            pf = jnp.dot(tril, jnp.concatenate([hi[:, :dk], lo[:, :dk]], axis=1), preferred_element_type=F32)
            pb = jnp.dot(triu, jnp.concatenate([hi[:, dk:], lo[:, dk:]], axis=1), preferred_element_type=F32)
            bf = pf[:, :dk] + pf[:, dk:]
            bb = pb[:, :dk] + pb[:, dk:]
            rf, tf = bf[mid - 1:mid, :], bf[c_len - 1:c_len, :]
            rb, tb = bb[mid:mid + 1, :], bb[0:1, :]
            k = k_at(rows).astype(F32)
            dec_f = jnp.exp(bf - rf)
            dec_b = jnp.exp(bb - rb)
            ks_f = k * (1.0 / dec_f)
            ks_b = k * (1.0 / dec_b)
            et_dst[i] = jnp.concatenate([jnp.exp(tf), jnp.exp(tb)], axis=1)
            if with_query:
                q = q_at(rows).astype(F32) * q_scale
                qs_f = q * dec_f
                qs_b = q * dec_b
                qs_scr[rows, :] = jnp.concatenate([qs_f, qs_b], axis=1).astype(BF16)
                qe_scr[rows, :] = jnp.concatenate([qs_f * jnp.exp(rf), qs_b * jnp.exp(rb)], axis=1).astype(BF16)
                kst_scr[:dk, rows] = ks_f.T.astype(BF16)
                kst_scr[dk:, rows] = ks_b.T.astype(BF16)
            return jnp.concatenate([ks_f * jnp.exp(tf - rf), ks_b * jnp.exp(tb - rb)], axis=1).astype(BF16)

        def state_increment(i, kd):
            pt_dst[i] = lax.dot_general(v_at(rows_of(i)), kd, TN_DIMS, preferred_element_type=F32)

        return [scaled_operands, state_increment]

    ctx_stages = prepare_stages(kc_at, vc_at, hic_at, loc_at, etc_scr, ptc_scr, False)
    lat_stages = prepare_stages(k_at, v_at, hi_at, lo_at, et_scr, pt_scr, True)
    both = [lambda i, arg, c=c, l=l: c(i, arg) if i < n_ctx_chunks else l(i - n_ctx_chunks, arg)
            for c, l in zip(ctx_stages, lat_stages)]
    _run_staggered(both, n_ctx_chunks + n_chunks, lag=GLA_PREPARE_LAG)

    s_f = jnp.zeros((dv, dk), F32)
    s_b = jnp.zeros((dv, dk), F32)
    for c in range(n_ctx_chunks):
        s_f = s_f * etc_scr[c, :, :dk] + ptc_scr[c, :, :dk]
    for c in reversed(range(n_ctx_chunks)):
        s_b = s_b * etc_scr[c, :, dk:] + ptc_scr[c, :, dk:]
    for c in range(n_chunks):
        sst_scr[c, :dk, :] = s_f.T.astype(BF16)
        s_f = s_f * et_scr[c, :, :dk] + pt_scr[c, :, :dk]
    for c in reversed(range(n_chunks)):
        sst_scr[c, dk:, :] = s_b.T.astype(BF16)
        s_b = s_b * et_scr[c, :, dk:] + pt_scr[c, :, dk:]

    def intra_scores(i, _):
        qs = qs_scr[rows_of(i), :]
        att_f = jnp.dot(qs[:, :dk], kst_scr[:dk, rows_of(i)], preferred_element_type=F32)
        att_b = jnp.dot(qs[:, dk:], kst_scr[dk:, rows_of(i)], preferred_element_type=F32)
        return (jnp.where(lower, att_f, 0.0) + jnp.where(upper, att_b, 0.0)).astype(BF16)

    def outputs(i, att):
        rows = rows_of(i)
        o = jnp.dot(att, v_at(rows), preferred_element_type=F32)
        o = o + jnp.dot(qe_scr[rows, :], sst_scr[i], preferred_element_type=F32)
        return o, jnp.mean(o * o, axis=-1, keepdims=True)

    def normalise(i, o_ms):
        rows = rows_of(i)
        on = o_ms[0] * lax.rsqrt(o_ms[1] + EPS)
        o_ref[0, rows, :] = ((on * hg_ref[...]) * gs_at(rows).astype(F32)).astype(o_ref.dtype)

    _run_staggered([intra_scores, outputs, normalise], n_chunks, lag=GLA_EMIT_LAG)


def _gla_scan(proj, proj_ctx, head_gain, riders, *, d_model):
    b, h, t, _ = proj.shape
    tc = proj_ctx.shape[2] // b
    kt, vt = d_model // 2, d_model
    dk, dv = kt // h, vt // h
    assert h == GLA_HEADS and t % GLA_CHUNK == 0 and tc % GLA_CHUNK == 0 and proj_ctx.shape[:3] == (1, h, b * tc)
    assert dk == V7X_LANES and dv == V7X_MXU_DIM and 2 * dk == V7X_MXU_DIM
    n_chunks, n_ctx_chunks = t // GLA_CHUNK, tc // GLA_CHUNK
    head_block = lambda a: pl.BlockSpec((1, 1) + a.shape[2:], lambda i, j: (i, j, 0, 0))
    ctx_block = pl.BlockSpec((1, 1, tc, proj_ctx.shape[3]), lambda i, j: (0, j, i, 0))
    flat = [r.reshape(-1, r.shape[-1]) for r in riders]
    slabs = [f.shape[0] // (b * h) for f in flat]
    bf16_rows = 16
    assert all(f.shape[0] == n * b * h and n % bf16_rows == 0 for f, n in zip(flat, slabs))
    rider_specs = [pl.BlockSpec((n, f.shape[1]), lambda i, j: (i * h + j, 0)) for f, n in zip(flat, slabs)]
    in_specs = [head_block(proj), ctx_block, pl.BlockSpec((1, dv), lambda i, j: (0, 0))] + rider_specs
    tok = lambda width: pltpu.VMEM((t, width), BF16)
    increments = lambda n: [pltpu.VMEM((n, dv, 2 * dk), F32), pltpu.VMEM((n, 1, 2 * dk), F32)]
    out, *cast = pl.pallas_call(
        functools.partial(_gla_kernel, n_riders=len(flat), n_chunks=n_chunks, n_ctx_chunks=n_ctx_chunks, dk=dk,
                          dv=dv, q_scale=float(dk) ** -0.5),
        grid=(b, h),
        in_specs=in_specs,
        out_specs=[pl.BlockSpec((1, t, dv), lambda i, j: (i, 0, j))] + rider_specs,
        out_shape=[jax.ShapeDtypeStruct((b, t, vt), BF16)] + [jax.ShapeDtypeStruct(f.shape, BF16) for f in flat],
        scratch_shapes=[tok(2 * dk), tok(2 * dk), pltpu.VMEM((2 * dk, t), BF16)] + increments(n_chunks)
                       + increments(n_ctx_chunks) + [pltpu.VMEM((n_chunks, 2 * dk, dv), BF16)],
        compiler_params=pltpu.CompilerParams(
            dimension_semantics=("parallel", "parallel"),
            vmem_limit_bytes=_vmem_limit(48 * 1024 * 1024)),
        name="gla_scan",
    )(proj, proj_ctx, head_gain.reshape(1, dv), *flat)
    return out, [c.reshape(r.shape) for c, r in zip(cast, riders)]


def _rolling_steps(n_tiles, head, body):
    s = pl.program_id(0)
    pl.when(s == 0)(head)
    pl.when(jnp.logical_and(s > 0, s < n_tiles))(functools.partial(body, True))
    pl.when(s == n_tiles)(functools.partial(body, False))


def _rolling_slots():
    head_slot = pl.program_id(0) % 2
    return head_slot, 1 - head_slot


def _head_tile(s, n_tiles):
    return jnp.minimum(s, n_tiles - 1)


def _body_tile(s):
    return jnp.maximum(s - 1, 0)


def _shift_rows(u, k):
    z = jnp.zeros((abs(k), u.shape[1]), u.dtype)
    if k > 0:
        return jnp.concatenate([z, u[:-k]], axis=0)
    return jnp.concatenate([u[-k:], z], axis=0)


def _ffn_kernel(x_ref, pre_ref, mix_gate_ref, shift_ref, scale_ref, gate_ref, gain_ref, wmix_ref,
                wup_ref, cw_ref, cb_ref, wdn_ref, fin_ref, o_ref, x_scr, hn_scr, *, hidden, final_norm,
                n_tiles):
    rows_img, cols, d = x_ref.shape[1:]
    tm = rows_img * cols
    head_slot, body_slot = _rolling_slots()

    def head_matmul():
        return jnp.dot(pre_ref[0].reshape(tm, pre_ref.shape[3]), wmix_ref[0], preferred_element_type=F32)

    def head_finish(y):
        x = x_ref[0].reshape(tm, d) + mix_gate_ref[0] * y
        x_scr[head_slot] = x
        hn_scr[head_slot] = _modulated_norm(x, gain_ref[...], shift_ref[0], scale_ref[0]).astype(BF16)

    def conv(u, c0):
        w = cw_ref[:, c0:c0 + HID_CHUNK]
        return (_shift_rows(u, cols) * w[0:1] + u * w[1:2] + _shift_rows(u, -cols) * w[2:3]
                + cb_ref[:, c0:c0 + HID_CHUNK])

    def body(with_head):
        acts = []
        head_y = []

        def up_proj(j, _):
            c0 = j * HID_CHUNK
            hn = hn_scr[body_slot]
            u = (jnp.dot(hn, wup_ref[0, :, c0:c0 + HID_CHUNK], preferred_element_type=F32),
                 jnp.dot(hn, wup_ref[0, :, hidden + c0:hidden + c0 + HID_CHUNK], preferred_element_type=F32))
            if with_head and j == 1:
                head_y.append(head_matmul())
            return u

        def conv_gate(j, u):
            c0 = j * HID_CHUNK
            acts.append((conv(u[0], c0) * _silu(conv(u[1], hidden + c0))).astype(BF16))
            if with_head and j == 1:
                head_finish(head_y.pop())

        _run_staggered([up_proj, conv_gate], hidden // HID_CHUNK, newest_first=True)
        down = jnp.dot(jnp.concatenate(acts, axis=1), wdn_ref[0], preferred_element_type=F32)
        out = x_scr[body_slot] + gate_ref[0] * down
        if final_norm:
            out = out * lax.rsqrt(jnp.mean(out * out, axis=-1, keepdims=True) + EPS) * fin_ref[...]
        o_ref[0] = out.reshape(rows_img, cols, d)

    _rolling_steps(n_tiles, lambda: head_finish(head_matmul()), body)


def _mixer_out_conv_ffn(x, pre, mod, w_mix, gain, w_up, conv_w, conv_b, w_down, fin_gain, *, mix_layer, layer,
                        final_norm):
    b, t, d = x.shape
    hidden = w_down.shape[1]
    assert t % GRID_W == 0 and GRID_W % FFN_COLS == 0 and hidden % HID_CHUNK == 0 and hidden >= 2 * HID_CHUNK
    rows_img = t // GRID_W
    col_tiles = GRID_W // FFN_COLS
    n_tiles = b * col_tiles
    tm = rows_img * FFN_COLS
    x4 = x.reshape(b, rows_img, GRID_W, d)
    pre4 = pre.reshape(b, rows_img, GRID_W, pre.shape[2])
    head = lambda s: _head_tile(s, n_tiles)
    back = _body_tile
    head_tile = lambda width: pl.BlockSpec((1, rows_img, FFN_COLS, width),
                                           lambda s: (head(s) // col_tiles, 0, head(s) % col_tiles, 0))
    head_mod = lambda idx: pl.BlockSpec((1, 1, d), lambda s: (head(s) // col_tiles, 0, idx))
    const = _resident
    out = pl.pallas_call(
        functools.partial(_ffn_kernel, hidden=hidden, final_norm=final_norm, n_tiles=n_tiles),
        grid=(n_tiles + 1,),
        in_specs=[head_tile(d), head_tile(pre.shape[2]), head_mod(2), head_mod(3), head_mod(4),
                  pl.BlockSpec((1, 1, d), lambda s: (back(s) // col_tiles, 0, 5)),
                  const((1, d)), _resident_layer(w_mix.shape, mix_layer), _resident_layer(w_up.shape, layer),
                  const(conv_w.shape), const((1, 2 * hidden)), _resident_layer(w_down.shape, layer),
                  const((1, d))],
        out_specs=pl.BlockSpec((1, rows_img, FFN_COLS, d),
                               lambda s: (back(s) // col_tiles, 0, back(s) % col_tiles, 0)),
        out_shape=jax.ShapeDtypeStruct(x4.shape, F32),
        scratch_shapes=[pltpu.VMEM((2, tm, d), F32), pltpu.VMEM((2, tm, d), BF16)],
        compiler_params=pltpu.CompilerParams(
            dimension_semantics=("arbitrary",),
            vmem_limit_bytes=_vmem_limit(60 * 1024 * 1024)),
        name="conv_ffn",
    )(x4, pre4, mod, mod, mod, mod, gain.reshape(1, d), w_mix, w_up, conv_w, conv_b.reshape(1, 2 * hidden),
      w_down, fin_gain.reshape(1, d))
    return out.reshape(b, t, d)


def _short_conv_kernel(x_ref, shift_ref, scale_ref, gain_ref, win_ref, cw_ref, o_ref, *, width):
    x = x_ref[0]
    tm = x.shape[0]
    hn = _modulated_norm(x, gain_ref[...], shift_ref[0], scale_ref[0]).astype(BF16)
    pos = lax.broadcasted_iota(jnp.int32, (tm, SC_CHUNK), 0) % GRID_W
    has_left = pos != 0
    has_right = pos != GRID_W - 1
    for c0 in range(0, width, SC_CHUNK):
        cg, v, bg = (jnp.dot(hn, win_ref[0, :, base + c0:base + c0 + SC_CHUNK], preferred_element_type=F32)
                     for base in (width, 2 * width, 0))
        u = cg * v
        w = cw_ref[:, c0:c0 + SC_CHUNK]
        left = jnp.where(has_left, pltpu.roll(u, 1, axis=0), 0.0)
        right = jnp.where(has_right, pltpu.roll(u, tm - 1, axis=0), 0.0)
        o_ref[0, :, c0:c0 + SC_CHUNK] = (bg * (left * w[0:1] + u * w[1:2] + right * w[2:3])).astype(o_ref.dtype)


def _short_conv(x, mod, gain, w_in, conv_w, *, layer, tm):
    b, t, d = x.shape
    width = conv_w.shape[1]
    assert t % tm == 0 and tm % GRID_W == 0 and width % SC_CHUNK == 0
    mod_spec = lambda idx: pl.BlockSpec((1, 1, d), lambda i, j: (i, 0, idx))
    const = _resident
    tile = pl.BlockSpec((1, tm, d), lambda i, j: (i, j, 0))
    return pl.pallas_call(
        functools.partial(_short_conv_kernel, width=width),
        grid=(b, t // tm),
        in_specs=[tile, mod_spec(0), mod_spec(1), const((1, d)), _resident_layer(w_in.shape, layer),
                  const(conv_w.shape)],
        out_specs=pl.BlockSpec((1, tm, width), lambda i, j: (i, j, 0)),
        out_shape=jax.ShapeDtypeStruct((b, t, width), BF16),
        compiler_params=pltpu.CompilerParams(
            dimension_semantics=("parallel", "parallel"),
            vmem_limit_bytes=_vmem_limit(56 * 1024 * 1024)),
        name="short_conv",
    )(x, mod, mod, gain.reshape(1, d), w_in, conv_w)


def kernel(x, c, ctx, c_ctx, ada_w, ada_b, norm_mix, norm_ffn, gla_w_in, gla_w_a2, gla_b_a,
           gla_head_norm, gla_w_out, sc_w_in, sc_conv_w, sc_w_out, ffn_w_up, ffn_conv_w,
           ffn_conv_b, ffn_w_down, final_norm):
    b, t, d = x.shape
    depth = ada_w.shape[0]
    assert depth == 2 and gla_w_in.shape[0] == 1 and sc_w_in.shape[0] == 1
    h = GLA_HEADS
    kt, vt, rank = d // 2, d, GLA_GATE_RANK
    dk = kt // h

    pad = (-(b + 1)) % 8
    cvec = jnp.concatenate([c, c_ctx[None, :], jnp.zeros((pad, d), F32)], axis=0)
    mods = _ada_mod(cvec, ada_w, ada_b)
    mod_x = [mods[i, :b].reshape(b, 1, N_MOD * d) for i in range(depth)]
    mod_ctx = mods[0, b:b + 1].reshape(1, 1, N_MOD * d)

    n_in = gla_w_in.shape[2]
    n_pad = (-n_in) % V7X_MXU_DIM
    w_in = jnp.pad(gla_w_in[0], ((0, 0), (0, n_pad))).astype(BF16)
    a2 = gla_w_a2[0].reshape(2, rank, h, dk)
    w2 = jnp.zeros((h, V7X_MXU_DIM, 2 * dk), F32)
    w2 = w2.at[:, :rank, :dk].set(a2[0].transpose(1, 0, 2))
    w2 = w2.at[:, rank:2 * rank, dk:].set(a2[1].transpose(1, 0, 2)).astype(BF16)
    w2 = w2.transpose(1, 0, 2).reshape(V7X_MXU_DIM, 2 * kt)
    ba = gla_b_a[0].reshape(2, h, dk).transpose(1, 0, 2).reshape(1, 2 * kt)

    ctx_flat = ctx.reshape(1, b * ctx.shape[1], d)
    proj_ctx = _gla_in_proj(ctx_flat, mod_ctx, norm_mix[0], w_in, w2, ba, kt=kt, vt=vt,
                            tm=math.gcd(ctx_flat.shape[1], TOKEN_TILE), with_query=False)
    proj_x = _gla_in_proj(x, mod_x[0], norm_mix[0], w_in, w2, ba, kt=kt, vt=vt, tm=TOKEN_TILE, with_query=True)
    o, (w_up, w_down, w_mix0, w_sc_in, w_mix1) = _gla_scan(
        proj_x, proj_ctx, gla_head_norm[0], [ffn_w_up, ffn_w_down, gla_w_out, sc_w_in, sc_w_out], d_model=d)
    hcur = _mixer_out_conv_ffn(x, o, mod_x[0], w_mix0, norm_ffn[0], w_up, ffn_conv_w[0], ffn_conv_b[0], w_down,
                               final_norm, mix_layer=0, layer=0, final_norm=False)

    y = _short_conv(hcur, mod_x[1], norm_mix[1], w_sc_in, sc_conv_w[0], layer=0, tm=TOKEN_TILE)
    return _mixer_out_conv_ffn(hcur, y, mod_x[1], w_mix1, norm_ffn[1], w_up, ffn_conv_w[1], ffn_conv_b[1], w_down,
                               final_norm, mix_layer=0, layer=1, final_norm=True)
```

```python
import functools
import math

import jax
import jax.numpy as jnp
from jax import lax
from jax.experimental import pallas as pl
from jax.experimental.pallas import tpu as pltpu

F32 = jnp.float32
BF16 = jnp.bfloat16

EPS = 1e-6
GRID_W = 64
N_MOD = 6
GLA_HEADS = 4
GLA_GATE_RANK = 16
GLA_GATE_TAU = 16.0

V7X_LANES = 128
V7X_MXU_DIM = 256
V7X_VMEM_BYTES = 64 * 1024 * 1024

GLA_CHUNK = 128
GLA_PREPARE_LAG = 2
GLA_EMIT_LAG = 4
TOKEN_TILE = 1024
FFN_COLS = 16
HID_CHUNK = 512
SC_CHUNK = 256

TN_DIMS = (((0,), (0,)), ((), ()))


def _vmem_limit(nbytes):
    return int(min(nbytes, V7X_VMEM_BYTES - 4 * 1024 * 1024))


def _resident(shape):
    return pl.BlockSpec(shape, lambda *_: (0,) * len(shape), pipeline_mode=pl.Buffered(1))


def _resident_layer(stacked_shape, layer):
    return pl.BlockSpec((1,) + tuple(stacked_shape[1:]), lambda *_: (layer, 0, 0), pipeline_mode=pl.Buffered(1))


def _modulated_norm(x, gain, shift, scale):
    y = x * lax.rsqrt(jnp.mean(x * x, axis=-1, keepdims=True) + EPS)
    return y * (gain * (1.0 + scale)) + shift


def _silu(x):
    return x * jax.nn.sigmoid(x)


def _run_staggered(stages, n_items, newest_first=False, lag=1):
    handoff = [dict() for _ in stages]
    for t in range(n_items + (len(stages) - 1) * lag):
        order = range(len(stages)) if newest_first else reversed(range(len(stages)))
        for s in order:
            i = t - s * lag
            if 0 <= i < n_items:
                arg = handoff[s - 1].pop(i) if s else None
                handoff[s][i] = stages[s](i, arg)


def _ada_kernel(c_ref, w_ref, b_ref, o_ref):
    c = c_ref[...]
    sc = _silu(c).astype(BF16)
    o_ref[0] = jnp.dot(sc, w_ref[0].astype(BF16), preferred_element_type=F32) + b_ref[0]


def _ada_mod(cvec, ada_w, ada_b):
    depth, d, n = ada_w.shape
    rows = cvec.shape[0]
    tn = n // 4
    return pl.pallas_call(
        _ada_kernel,
        grid=(depth, n // tn),
        in_specs=[
            pl.BlockSpec((rows, d), lambda i, j: (0, 0)),
            pl.BlockSpec((1, d, tn), lambda i, j: (i, 0, j)),
            pl.BlockSpec((1, 1, tn), lambda i, j: (i, 0, j)),
        ],
        out_specs=pl.BlockSpec((1, rows, tn), lambda i, j: (i, 0, j)),
        out_shape=jax.ShapeDtypeStruct((depth, rows, n), F32),
        compiler_params=pltpu.CompilerParams(
            dimension_semantics=("parallel", "parallel"),
            vmem_limit_bytes=_vmem_limit(4 * d * tn * 4)),
        name="ada_mod",
    )(cvec, ada_w, ada_b.reshape(depth, 1, n))


def _gla_proj_kernel(x_ref, shift_ref, scale_ref, gain_ref, w_ref, w2_ref, ba_ref, o_ref, *, groups, w_low,
                     o_dec):
    hn = _modulated_norm(x_ref[0], gain_ref[...], shift_ref[0], scale_ref[0]).astype(BF16)
    n_chunk = 2 * V7X_MXU_DIM

    def proj(n0, width):
        return jnp.dot(hn, w_ref[:, n0:n0 + width], preferred_element_type=F32)

    n_dec = w2_ref.shape[1]
    piece = V7X_MXU_DIM
    a_low = proj(w_low, w2_ref.shape[0]).astype(BF16)

    def log_decay_piece(head):
        c0 = head * piece
        z = jnp.dot(a_low, w2_ref[:, c0:c0 + piece], preferred_element_type=F32) + ba_ref[:, c0:c0 + piece]
        la = (jnp.minimum(z, 0.0) - jnp.log(1.0 + jnp.exp(-jnp.abs(z)))) * (1.0 / GLA_GATE_TAU)
        hi = la.astype(BF16)
        o_ref[0, head, :, o_dec:o_dec + piece] = hi
        o_ref[0, head, :, o_dec + piece:o_dec + 2 * piece] = (la - hi.astype(F32)).astype(BF16)

    pieces = list(range(n_dec // piece))
    for w0, o0, width, head0, gated in groups:
        y = proj(w0, n_chunk)
        y = (_silu(y) if gated else y).astype(o_ref.dtype)
        for i in range(n_chunk // width):
            o_ref[0, head0 + i, :, o0:o0 + width] = y[:, i * width:(i + 1) * width]
        if pieces:
            log_decay_piece(pieces.pop(0))
    for head in pieces:
        log_decay_piece(head)


def _gla_in_proj(x, mod, gain, w, w2, ba, *, kt, vt, tm, with_query):
    b, t, d = x.shape
    h = GLA_HEADS
    dk, dv = kt // h, vt // h
    n_chunk = 2 * V7X_MXU_DIM
    w_low = 2 * kt + 2 * vt
    fields = [(kt, kt, dk, False), (2 * kt, vt, dv, False)]
    if with_query:
        fields = [(0, kt, dk, False)] + fields + [(2 * kt + vt, vt, dv, True)]
    groups, o0 = [], 0
    for w0, width, per_head, gated in fields:
        assert width % n_chunk == 0 and n_chunk % per_head == 0
        groups += [(w0 + c, o0, per_head, c // per_head, gated) for c in range(0, width, n_chunk)]
        o0 += per_head
    groups.sort(key=lambda g: not g[4])
    o_dec, n_out = o0, o0 + 4 * dk
    per_batch = mod.shape[0] != 1
    mod_map = lambda idx: (lambda i, j: (i if per_batch else 0, 0, idx))
    assert t % tm == 0 and w.shape[1] == w_low + w2.shape[0] and w2.shape[1] == 2 * kt and 2 * dk == V7X_MXU_DIM
    return pl.pallas_call(
        functools.partial(_gla_proj_kernel, groups=tuple(groups), w_low=w_low, o_dec=o_dec),
        grid=(b, t // tm),
        in_specs=[
            pl.BlockSpec((1, tm, d), lambda i, j: (i, j, 0)),
            pl.BlockSpec((1, 1, d), mod_map(0)),
            pl.BlockSpec((1, 1, d), mod_map(1)),
            _resident((1, d)),
            _resident(w.shape),
            _resident(w2.shape),
            _resident(ba.shape),
        ],
        out_specs=pl.BlockSpec((1, h, tm, n_out), lambda i, j: (i, 0, j, 0)),
        out_shape=jax.ShapeDtypeStruct((b, h, t, n_out), BF16),
        compiler_params=pltpu.CompilerParams(
            dimension_semantics=("parallel", "parallel"),
            vmem_limit_bytes=_vmem_limit(48 * 1024 * 1024)),
        name="gla_in_proj",
    )(x, mod, mod, gain.reshape(1, d), w, w2, ba)


def _gla_kernel(p_ref, pc_ref, hg_ref, *refs, n_riders, n_chunks, n_ctx_chunks, dk, dv, q_scale):
    rider_in, (o_ref, *rider_out) = refs[:n_riders], refs[n_riders:2 * n_riders + 1]
    qs_scr, qe_scr, kst_scr, pt_scr, et_scr, ptc_scr, etc_scr, sst_scr = refs[2 * n_riders + 1:]
    for src, dst in zip(rider_in, rider_out):
        dst[...] = src[...].astype(dst.dtype)
    c_len = GLA_CHUNK
    mid = c_len // 2

    def columns(ref, widths):
        starts = [sum(widths[:i]) for i in range(len(widths))]
        return [lambda rows, c0=c0, w=w: ref[0, 0, rows, c0:c0 + w] for c0, w in zip(starts, widths)]

    q_at, k_at, v_at, gs_at, hi_at, lo_at = columns(p_ref, [dk, dk, dv, dv, 2 * dk, 2 * dk])
    kc_at, vc_at, hic_at, loc_at = columns(pc_ref, [dk, dv, 2 * dk, 2 * dk])

    row = lax.broadcasted_iota(jnp.int32, (c_len, c_len), 0)
    col = lax.broadcasted_iota(jnp.int32, (c_len, c_len), 1)
    lower = row >= col
    upper = row <= col
    tril = jnp.where(lower, 1.0, 0.0).astype(BF16)
    triu = jnp.where(upper, 1.0, 0.0).astype(BF16)

    def rows_of(i):
        return slice(i * c_len, (i + 1) * c_len)

    def prepare_stages(k_at, v_at, hi_at, lo_at, et_dst, pt_dst, with_query):
        def scaled_operands(i, _):
            rows = rows_of(i)
            hi, lo = hi_at(rows), lo_at(rows)
            pf = jnp.dot(tril, jnp.concatenate([hi[:, :dk], lo[:, :dk]], axis=1), preferred_element_type=F32)
            pb = jnp.dot(triu, jnp.concatenate([hi[:, dk:], lo[:, dk:]], axis=1), preferred_element_type=F32)
            bf = pf[:, :dk] + pf[:, dk:]
            bb = pb[:, :dk] + pb[:, dk:]
            rf, tf = bf[mid - 1:mid, :], bf[c_len - 1:c_len, :]
            rb, tb = bb[mid:mid + 1, :], bb[0:1, :]
            k = k_at(rows).astype(F32)
            dec_f = jnp.exp(bf - rf)
            dec_b = jnp.exp(bb - rb)
            ks_f = k * (1.0 / dec_f)
            ks_b = k * (1.0 / dec_b)
            et_dst[i] = jnp.concatenate([jnp.exp(tf), jnp.exp(tb)], axis=1)
            if with_query:
                q = q_at(rows).astype(F32) * q_scale
                qs_f = q * dec_f
                qs_b = q * dec_b
                qs_scr[rows, :] = jnp.concatenate([qs_f, qs_b], axis=1).astype(BF16)
                qe_scr[rows, :] = jnp.concatenate([qs_f * jnp.exp(rf), qs_b * jnp.exp(rb)], axis=1).astype(BF16)
                kst_scr[:dk, rows] = ks_f.T.astype(BF16)
                kst_scr[dk:, rows] = ks_b.T.astype(BF16)
            return jnp.concatenate([ks_f * jnp.exp(tf - rf), ks_b * jnp.exp(tb - rb)], axis=1).astype(BF16)

        def state_increment(i, kd):
            pt_dst[i] = lax.dot_general(v_at(rows_of(i)), kd, TN_DIMS, preferred_element_type=F32)

        return [scaled_operands, state_increment]

    ctx_stages = prepare_stages(kc_at, vc_at, hic_at, loc_at, etc_scr, ptc_scr, False)
    lat_stages = prepare_stages(k_at, v_at, hi_at, lo_at, et_scr, pt_scr, True)
    both = [lambda i, arg, c=c, l=l: c(i, arg) if i < n_ctx_chunks else l(i - n_ctx_chunks, arg)
            for c, l in zip(ctx_stages, lat_stages)]
    _run_staggered(both, n_ctx_chunks + n_chunks, lag=GLA_PREPARE_LAG)

    s_f = jnp.zeros((dv, dk), F32)
    s_b = jnp.zeros((dv, dk), F32)
    for c in range(n_ctx_chunks):
        s_f = s_f * etc_scr[c, :, :dk] + ptc_scr[c, :, :dk]
    for c in reversed(range(n_ctx_chunks)):
        s_b = s_b * etc_scr[c, :, dk:] + ptc_scr[c, :, dk:]
    for c in range(n_chunks):
        sst_scr[c, :dk, :] = s_f.T.astype(BF16)
        s_f = s_f * et_scr[c, :, :dk] + pt_scr[c, :, :dk]
    for c in reversed(range(n_chunks)):
        sst_scr[c, dk:, :] = s_b.T.astype(BF16)
        s_b = s_b * et_scr[c, :, dk:] + pt_scr[c, :, dk:]

    def intra_scores(i, _):
        qs = qs_scr[rows_of(i), :]
        att_f = jnp.dot(qs[:, :dk], kst_scr[:dk, rows_of(i)], preferred_element_type=F32)
        att_b = jnp.dot(qs[:, dk:], kst_scr[dk:, rows_of(i)], preferred_element_type=F32)
        return (jnp.where(lower, att_f, 0.0) + jnp.where(upper, att_b, 0.0)).astype(BF16)

    def outputs(i, att):
        rows = rows_of(i)
        o = jnp.dot(att, v_at(rows), preferred_element_type=F32)
        o = o + jnp.dot(qe_scr[rows, :], sst_scr[i], preferred_element_type=F32)
        return o, jnp.mean(o * o, axis=-1, keepdims=True)

    def normalise(i, o_ms):
        rows = rows_of(i)
        on = o_ms[0] * lax.rsqrt(o_ms[1] + EPS)
        o_ref[0, rows, :] = ((on * hg_ref[...]) * gs_at(rows).astype(F32)).astype(o_ref.dtype)

    _run_staggered([intra_scores, outputs, normalise], n_chunks, lag=GLA_EMIT_LAG)


def _gla_scan(proj, proj_ctx, head_gain, riders, *, d_model):
    b, h, t, _ = proj.shape
    tc = proj_ctx.shape[2] // b
    kt, vt = d_model // 2, d_model
    dk, dv = kt // h, vt // h
    assert h == GLA_HEADS and t % GLA_CHUNK == 0 and tc % GLA_CHUNK == 0 and proj_ctx.shape[:3] == (1, h, b * tc)
    assert dk == V7X_LANES and dv == V7X_MXU_DIM and 2 * dk == V7X_MXU_DIM
    n_chunks, n_ctx_chunks = t // GLA_CHUNK, tc // GLA_CHUNK
    head_block = lambda a: pl.BlockSpec((1, 1) + a.shape[2:], lambda i, j: (i, j, 0, 0))
    ctx_block = pl.BlockSpec((1, 1, tc, proj_ctx.shape[3]), lambda i, j: (0, j, i, 0))
    flat = [r.reshape(-1, r.shape[-1]) for r in riders]
    slabs = [f.shape[0] // (b * h) for f in flat]
    bf16_rows = 16
    assert all(f.shape[0] == n * b * h and n % bf16_rows == 0 for f, n in zip(flat, slabs))
    rider_specs = [pl.BlockSpec((n, f.shape[1]), lambda i, j: (i * h + j, 0)) for f, n in zip(flat, slabs)]
    in_specs = [head_block(proj), ctx_block, pl.BlockSpec((1, dv), lambda i, j: (0, 0))] + rider_specs
    tok = lambda width: pltpu.VMEM((t, width), BF16)
    increments = lambda n: [pltpu.VMEM((n, dv, 2 * dk), F32), pltpu.VMEM((n, 1, 2 * dk), F32)]
    out, *cast = pl.pallas_call(
        functools.partial(_gla_kernel, n_riders=len(flat), n_chunks=n_chunks, n_ctx_chunks=n_ctx_chunks, dk=dk,
                          dv=dv, q_scale=float(dk) ** -0.5),
        grid=(b, h),
        in_specs=in_specs,
        out_specs=[pl.BlockSpec((1, t, dv), lambda i, j: (i, 0, j))] + rider_specs,
        out_shape=[jax.ShapeDtypeStruct((b, t, vt), BF16)] + [jax.ShapeDtypeStruct(f.shape, BF16) for f in flat],
        scratch_shapes=[tok(2 * dk), tok(2 * dk), pltpu.VMEM((2 * dk, t), BF16)] + increments(n_chunks)
                       + increments(n_ctx_chunks) + [pltpu.VMEM((n_chunks, 2 * dk, dv), BF16)],
        compiler_params=pltpu.CompilerParams(
            dimension_semantics=("parallel", "parallel"),
            vmem_limit_bytes=_vmem_limit(48 * 1024 * 1024)),
        name="gla_scan",
    )(proj, proj_ctx, head_gain.reshape(1, dv), *flat)
    return out, [c.reshape(r.shape) for c, r in zip(cast, riders)]


def _rolling_steps(n_tiles, head, body):
    s = pl.program_id(0)
    pl.when(s == 0)(head)
    pl.when(jnp.logical_and(s > 0, s < n_tiles))(functools.partial(body, True))
    pl.when(s == n_tiles)(functools.partial(body, False))


def _rolling_slots():
    head_slot = pl.program_id(0) % 2
    return head_slot, 1 - head_slot


def _head_tile(s, n_tiles):
    return jnp.minimum(s, n_tiles - 1)


def _body_tile(s):
    return jnp.maximum(s - 1, 0)


def _shift_rows(u, k):
    z = jnp.zeros((abs(k), u.shape[1]), u.dtype)
    if k > 0:
        return jnp.concatenate([z, u[:-k]], axis=0)
    return jnp.concatenate([u[-k:], z], axis=0)


def _ffn_kernel(x_ref, pre_ref, mix_gate_ref, shift_ref, scale_ref, gate_ref, gain_ref, wmix_ref,
                wup_ref, cw_ref, cb_ref, wdn_ref, fin_ref, o_ref, x_scr, hn_scr, *, hidden, final_norm,
                n_tiles):
    rows_img, cols, d = x_ref.shape[1:]
    tm = rows_img * cols
    head_slot, body_slot = _rolling_slots()

    def head_matmul():
        return jnp.dot(pre_ref[0].reshape(tm, pre_ref.shape[3]), wmix_ref[0], preferred_element_type=F32)

    def head_finish(y):
        x = x_ref[0].reshape(tm, d) + mix_gate_ref[0] * y
        x_scr[head_slot] = x
        hn_scr[head_slot] = _modulated_norm(x, gain_ref[...], shift_ref[0], scale_ref[0]).astype(BF16)

    def conv(u, c0):
        w = cw_ref[:, c0:c0 + HID_CHUNK]
        return (_shift_rows(u, cols) * w[0:1] + u * w[1:2] + _shift_rows(u, -cols) * w[2:3]
                + cb_ref[:, c0:c0 + HID_CHUNK])

    def body(with_head):
        acts = []
        head_y = []

        def up_proj(j, _):
            c0 = j * HID_CHUNK
            hn = hn_scr[body_slot]
            u = (jnp.dot(hn, wup_ref[0, :, c0:c0 + HID_CHUNK], preferred_element_type=F32),
                 jnp.dot(hn, wup_ref[0, :, hidden + c0:hidden + c0 + HID_CHUNK], preferred_element_type=F32))
            if with_head and j == 1:
                head_y.append(head_matmul())
            return u

        def conv_gate(j, u):
            c0 = j * HID_CHUNK
            acts.append((conv(u[0], c0) * _silu(conv(u[1], hidden + c0))).astype(BF16))
            if with_head and j == 1:
                head_finish(head_y.pop())

        _run_staggered([up_proj, conv_gate], hidden // HID_CHUNK, newest_first=True)
        down = jnp.dot(jnp.concatenate(acts, axis=1), wdn_ref[0], preferred_element_type=F32)
        out = x_scr[body_slot] + gate_ref[0] * down
        if final_norm:
            out = out * lax.rsqrt(jnp.mean(out * out, axis=-1, keepdims=True) + EPS) * fin_ref[...]
        o_ref[0] = out.reshape(rows_img, cols, d)

    _rolling_steps(n_tiles, lambda: head_finish(head_matmul()), body)


def _mixer_out_conv_ffn(x, pre, mod, w_mix, gain, w_up, conv_w, conv_b, w_down, fin_gain, *, mix_layer, layer,
                        final_norm):
    b, t, d = x.shape
    hidden = w_down.shape[1]
    assert t % GRID_W == 0 and GRID_W % FFN_COLS == 0 and hidden % HID_CHUNK == 0 and hidden >= 2 * HID_CHUNK
    rows_img = t // GRID_W
    col_tiles = GRID_W // FFN_COLS
    n_tiles = b * col_tiles
    tm = rows_img * FFN_COLS
    x4 = x.reshape(b, rows_img, GRID_W, d)
    pre4 = pre.reshape(b, rows_img, GRID_W, pre.shape[2])
    head = lambda s: _head_tile(s, n_tiles)
    back = _body_tile
    head_tile = lambda width: pl.BlockSpec((1, rows_img, FFN_COLS, width),
                                           lambda s: (head(s) // col_tiles, 0, head(s) % col_tiles, 0))
    head_mod = lambda idx: pl.BlockSpec((1, 1, d), lambda s: (head(s) // col_tiles, 0, idx))
    const = _resident
    out = pl.pallas_call(
        functools.partial(_ffn_kernel, hidden=hidden, final_norm=final_norm, n_tiles=n_tiles),
        grid=(n_tiles + 1,),
        in_specs=[head_tile(d), head_tile(pre.shape[2]), head_mod(2), head_mod(3), head_mod(4),
                  pl.BlockSpec((1, 1, d), lambda s: (back(s) // col_tiles, 0, 5)),
                  const((1, d)), _resident_layer(w_mix.shape, mix_layer), _resident_layer(w_up.shape, layer),
                  const(conv_w.shape), const((1, 2 * hidden)), _resident_layer(w_down.shape, layer),
                  const((1, d))],
        out_specs=pl.BlockSpec((1, rows_img, FFN_COLS, d),
                               lambda s: (back(s) // col_tiles, 0, back(s) % col_tiles, 0)),
        out_shape=jax.ShapeDtypeStruct(x4.shape, F32),
        scratch_shapes=[pltpu.VMEM((2, tm, d), F32), pltpu.VMEM((2, tm, d), BF16)],
        compiler_params=pltpu.CompilerParams(
            dimension_semantics=("arbitrary",),
            vmem_limit_bytes=_vmem_limit(60 * 1024 * 1024)),
        name="conv_ffn",
    )(x4, pre4, mod, mod, mod, mod, gain.reshape(1, d), w_mix, w_up, conv_w, conv_b.reshape(1, 2 * hidden),
      w_down, fin_gain.reshape(1, d))
    return out.reshape(b, t, d)


def _short_conv_kernel(x_ref, shift_ref, scale_ref, gain_ref, win_ref, cw_ref, o_ref, *, width):
    x = x_ref[0]
    tm = x.shape[0]
    hn = _modulated_norm(x, gain_ref[...], shift_ref[0], scale_ref[0]).astype(BF16)
    pos = lax.broadcasted_iota(jnp.int32, (tm, SC_CHUNK), 0) % GRID_W
    has_left = pos != 0
    has_right = pos != GRID_W - 1
    for c0 in range(0, width, SC_CHUNK):
        cg, v, bg = (jnp.dot(hn, win_ref[0, :, base + c0:base + c0 + SC_CHUNK], preferred_element_type=F32)
                     for base in (width, 2 * width, 0))
        u = cg * v
        w = cw_ref[:, c0:c0 + SC_CHUNK]
        left = jnp.where(has_left, pltpu.roll(u, 1, axis=0), 0.0)
        right = jnp.where(has_right, pltpu.roll(u, tm - 1, axis=0), 0.0)
        o_ref[0, :, c0:c0 + SC_CHUNK] = (bg * (left * w[0:1] + u * w[1:2] + right * w[2:3])).astype(o_ref.dtype)


def _short_conv(x, mod, gain, w_in, conv_w, *, layer, tm):
    b, t, d = x.shape
    width = conv_w.shape[1]
    assert t % tm == 0 and tm % GRID_W == 0 and width % SC_CHUNK == 0
    mod_spec = lambda idx: pl.BlockSpec((1, 1, d), lambda i, j: (i, 0, idx))
    const = _resident
    tile = pl.BlockSpec((1, tm, d), lambda i, j: (i, j, 0))
    return pl.pallas_call(
        functools.partial(_short_conv_kernel, width=width),
        grid=(b, t // tm),
        in_specs=[tile, mod_spec(0), mod_spec(1), const((1, d)), _resident_layer(w_in.shape, layer),
                  const(conv_w.shape)],
        out_specs=pl.BlockSpec((1, tm, width), lambda i, j: (i, j, 0)),
        out_shape=jax.ShapeDtypeStruct((b, t, width), BF16),
        compiler_params=pltpu.CompilerParams(
            dimension_semantics=("parallel", "parallel"),
            vmem_limit_bytes=_vmem_limit(56 * 1024 * 1024)),
        name="short_conv",
    )(x, mod, mod, gain.reshape(1, d), w_in, conv_w)


def kernel(x, c, ctx, c_ctx, ada_w, ada_b, norm_mix, norm_ffn, gla_w_in, gla_w_a2, gla_b_a,
           gla_head_norm, gla_w_out, sc_w_in, sc_conv_w, sc_w_out, ffn_w_up, ffn_conv_w,
           ffn_conv_b, ffn_w_down, final_norm):
    b, t, d = x.shape
    depth = ada_w.shape[0]
    assert depth == 2 and gla_w_in.shape[0] == 1 and sc_w_in.shape[0] == 1
    h = GLA_HEADS
    kt, vt, rank = d // 2, d, GLA_GATE_RANK
    dk = kt // h

    pad = (-(b + 1)) % 8
    cvec = jnp.concatenate([c, c_ctx[None, :], jnp.zeros((pad, d), F32)], axis=0)
    mods = _ada_mod(cvec, ada_w, ada_b)
    mod_x = [mods[i, :b].reshape(b, 1, N_MOD * d) for i in range(depth)]
    mod_ctx = mods[0, b:b + 1].reshape(1, 1, N_MOD * d)

    n_in = gla_w_in.shape[2]
    n_pad = (-n_in) % V7X_MXU_DIM
    w_in = jnp.pad(gla_w_in[0], ((0, 0), (0, n_pad))).astype(BF16)
    a2 = gla_w_a2[0].reshape(2, rank, h, dk)
    w2 = jnp.zeros((h, V7X_MXU_DIM, 2 * dk), F32)
    w2 = w2.at[:, :rank, :dk].set(a2[0].transpose(1, 0, 2))
    w2 = w2.at[:, rank:2 * rank, dk:].set(a2[1].transpose(1, 0, 2)).astype(BF16)
    w2 = w2.transpose(1, 0, 2).reshape(V7X_MXU_DIM, 2 * kt)
    ba = gla_b_a[0].reshape(2, h, dk).transpose(1, 0, 2).reshape(1, 2 * kt)

    ctx_flat = ctx.reshape(1, b * ctx.shape[1], d)
    proj_ctx = _gla_in_proj(ctx_flat, mod_ctx, norm_mix[0], w_in, w2, ba, kt=kt, vt=vt,
                            tm=math.gcd(ctx_flat.shape[1], TOKEN_TILE), with_query=False)
    proj_x = _gla_in_proj(x, mod_x[0], norm_mix[0], w_in, w2, ba, kt=kt, vt=vt, tm=TOKEN_TILE, with_query=True)
    o, (w_up, w_down, w_mix0, w_sc_in, w_mix1) = _gla_scan(
        proj_x, proj_ctx, gla_head_norm[0], [ffn_w_up, ffn_w_down, gla_w_out, sc_w_in, sc_w_out], d_model=d)
    hcur = _mixer_out_conv_ffn(x, o, mod_x[0], w_mix0, norm_ffn[0], w_up, ffn_conv_w[0], ffn_conv_b[0], w_down,
                               final_norm, mix_layer=0, layer=0, final_norm=False)

    y = _short_conv(hcur, mod_x[1], norm_mix[1], w_sc_in, sc_conv_w[0], layer=0, tm=TOKEN_TILE)
    return _mixer_out_conv_ffn(hcur, y, mod_x[1], w_mix1, norm_ffn[1], w_up, ffn_conv_w[1], ffn_conv_b[1], w_down,
                               final_norm, mix_layer=0, layer=1, final_norm=True)
```

```python
import functools
import math

import jax
import jax.numpy as jnp
from jax import lax
from jax.experimental import pallas as pl
from jax.experimental.pallas import tpu as pltpu

F32 = jnp.float32
BF16 = jnp.bfloat16

EPS = 1e-6
GRID_W = 64
N_MOD = 6
GLA_HEADS = 4
GLA_GATE_RANK = 16
GLA_GATE_TAU = 16.0
CONV_WIDTH = 3

V7X_LANES = 128
V7X_MXU_DIM = 256
V7X_VMEM_BYTES = 64 * 1024 * 1024

GLA_CHUNK = 128
GLA_PREPARE_LAG = 2
GLA_EMIT_LAG = 4
TOKEN_TILE = 1024
FFN_COLS = 32
HID_CHUNK = 512
SC_CHUNK = 256

TN_DIMS = (((0,), (0,)), ((), ()))


def _vmem_limit(nbytes):
    return int(min(nbytes, V7X_VMEM_BYTES - 4 * 1024 * 1024))


def _resident(shape):
    return pl.BlockSpec(shape, lambda *_: (0,) * len(shape), pipeline_mode=pl.Buffered(1))


def _resident_layer(stacked_shape, layer):
    return pl.BlockSpec((1,) + tuple(stacked_shape[1:]), lambda *_: (layer, 0, 0), pipeline_mode=pl.Buffered(1))


def _modulated_norm(x, gain, shift, scale):
    y = x * lax.rsqrt(jnp.mean(x * x, axis=-1, keepdims=True) + EPS)
    return y * (gain * (1.0 + scale)) + shift


def _silu(x):
    return x * jax.nn.sigmoid(x)


def _run_staggered(stages, n_items, newest_first=False, lag=1):
    handoff = [dict() for _ in stages]
    for t in range(n_items + (len(stages) - 1) * lag):
        order = range(len(stages)) if newest_first else reversed(range(len(stages)))
        for s in order:
            i = t - s * lag
            if 0 <= i < n_items:
                arg = handoff[s - 1].pop(i) if s else None
                handoff[s][i] = stages[s](i, arg)


def _ada_kernel(c_ref, w_ref, b_ref, o_ref):
    c = c_ref[...]
    sc = _silu(c).astype(BF16)
    o_ref[0] = jnp.dot(sc, w_ref[0].astype(BF16), preferred_element_type=F32) + b_ref[0]


def _ada_mod(cvec, ada_w, ada_b):
    depth, d, n = ada_w.shape
    rows = cvec.shape[0]
    tn = n // 4
    return pl.pallas_call(
        _ada_kernel,
        grid=(depth, n // tn),
        in_specs=[
            pl.BlockSpec((rows, d), lambda i, j: (0, 0)),
            pl.BlockSpec((1, d, tn), lambda i, j: (i, 0, j)),
            pl.BlockSpec((1, 1, tn), lambda i, j: (i, 0, j)),
        ],
        out_specs=pl.BlockSpec((1, rows, tn), lambda i, j: (i, 0, j)),
        out_shape=jax.ShapeDtypeStruct((depth, rows, n), F32),
        compiler_params=pltpu.CompilerParams(
            dimension_semantics=("parallel", "parallel"),
            vmem_limit_bytes=_vmem_limit(4 * d * tn * 4)),
        name="ada_mod",
    )(cvec, ada_w, ada_b.reshape(depth, 1, n))


def _gla_proj_kernel(x_ref, shift_ref, scale_ref, gain_ref, w_ref, w2_ref, ba_ref, o_ref, *, groups, w_low,
                     o_dec):
    hn = _modulated_norm(x_ref[0], gain_ref[...], shift_ref[0], scale_ref[0]).astype(BF16)
    n_chunk = 2 * V7X_MXU_DIM

    def proj(n0, width):
        return jnp.dot(hn, w_ref[:, n0:n0 + width], preferred_element_type=F32)

    n_dec = w2_ref.shape[1]
    piece = V7X_MXU_DIM
    a_low = proj(w_low, w2_ref.shape[0]).astype(BF16)

    def log_decay_piece(head):
        c0 = head * piece
        z = jnp.dot(a_low, w2_ref[:, c0:c0 + piece], preferred_element_type=F32) + ba_ref[:, c0:c0 + piece]
        la = (jnp.minimum(z, 0.0) - jnp.log(1.0 + jnp.exp(-jnp.abs(z)))) * (1.0 / GLA_GATE_TAU)
        hi = la.astype(BF16)
        o_ref[0, head, :, o_dec:o_dec + piece] = hi
        o_ref[0, head, :, o_dec + piece:o_dec + 2 * piece] = (la - hi.astype(F32)).astype(BF16)

    pieces = list(range(n_dec // piece))
    for w0, o0, width, head0, gated in groups:
        y = proj(w0, n_chunk)
        y = (_silu(y) if gated else y).astype(o_ref.dtype)
        for i in range(n_chunk // width):
            o_ref[0, head0 + i, :, o0:o0 + width] = y[:, i * width:(i + 1) * width]
        if pieces:
            log_decay_piece(pieces.pop(0))
    for head in pieces:
        log_decay_piece(head)


def _gla_in_proj(x, mod, gain, w, w2, ba, *, kt, vt, tm, with_query):
    b, t, d = x.shape
    h = GLA_HEADS
    dk, dv = kt // h, vt // h
    n_chunk = 2 * V7X_MXU_DIM
    w_low = 2 * kt + 2 * vt
    fields = [(kt, kt, dk, False), (2 * kt, vt, dv, False)]
    if with_query:
        fields = [(0, kt, dk, False)] + fields + [(2 * kt + vt, vt, dv, True)]
    groups, o0 = [], 0
    for w0, width, per_head, gated in fields:
        assert width % n_chunk == 0 and n_chunk % per_head == 0
        groups += [(w0 + c, o0, per_head, c // per_head, gated) for c in range(0, width, n_chunk)]
        o0 += per_head
    groups.sort(key=lambda g: not g[4])
    o_dec, n_out = o0, o0 + 4 * dk
    per_batch = mod.shape[0] != 1
    mod_map = lambda idx: (lambda i, j: (i if per_batch else 0, 0, idx))
    assert t % tm == 0 and w.shape[1] == w_low + w2.shape[0] and w2.shape[1] == 2 * kt and 2 * dk == V7X_MXU_DIM
    return pl.pallas_call(
        functools.partial(_gla_proj_kernel, groups=tuple(groups), w_low=w_low, o_dec=o_dec),
        grid=(b, t // tm),
        in_specs=[
            pl.BlockSpec((1, tm, d), lambda i, j: (i, j, 0)),
            pl.BlockSpec((1, 1, d), mod_map(0)),
            pl.BlockSpec((1, 1, d), mod_map(1)),
            _resident((1, d)),
            _resident(w.shape),
            _resident(w2.shape),
            _resident(ba.shape),
        ],
        out_specs=pl.BlockSpec((1, h, tm, n_out), lambda i, j: (i, 0, j, 0)),
        out_shape=jax.ShapeDtypeStruct((b, h, t, n_out), BF16),
        compiler_params=pltpu.CompilerParams(
            dimension_semantics=("parallel", "parallel"),
            vmem_limit_bytes=_vmem_limit(48 * 1024 * 1024)),
        name="gla_in_proj",
    )(x, mod, mod, gain.reshape(1, d), w, w2, ba)


def _gla_kernel(p_ref, pc_ref, hg_ref, *refs, n_riders, n_chunks, n_ctx_chunks, dk, dv, q_scale):
    rider_in, (o_ref, *rider_out) = refs[:n_riders], refs[n_riders:2 * n_riders + 1]
    qs_scr, qe_scr, kst_scr, pt_scr, et_scr, ptc_scr, etc_scr, sst_scr = refs[2 * n_riders + 1:]
    for src, dst in zip(rider_in, rider_out):
        dst[...] = src[...].astype(dst.dtype)
    c_len = GLA_CHUNK
    mid = c_len // 2

    def columns(ref, widths):
        starts = [sum(widths[:i]) for i in range(len(widths))]
        return [lambda rows, c0=c0, w=w: ref[0, 0, rows, c0:c0 + w] for c0, w in zip(starts, widths)]

    q_at, k_at, v_at, gs_at, hi_at, lo_at = columns(p_ref, [dk, dk, dv, dv, 2 * dk, 2 * dk])
    kc_at, vc_at, hic_at, loc_at = columns(pc_ref, [dk, dv, 2 * dk, 2 * dk])

    row = lax.broadcasted_iota(jnp.int32, (c_len, c_len), 0)
    col = lax.broadcasted_iota(jnp.int32, (c_len, c_len), 1)
    lower = row >= col
    upper = row <= col
    tril = jnp.where(lower, 1.0, 0.0).astype(BF16)
    triu = jnp.where(upper, 1.0, 0.0).astype(BF16)

    def rows_of(i):
        return slice(i * c_len, (i + 1) * c_len)

    def prepare_stages(k_at, v_at, hi_at, lo_at, et_dst, pt_dst, with_query):
        def scaled_operands(i, _):
            rows = rows_of(i)
            hi, lo = hi_at(rows), lo_at(rows)
            pf = jnp.dot(tril, jnp.concatenate([hi[:, :dk], lo[:, :dk]], axis=1), preferred_element_type=F32)
            pb = jnp.dot(triu, jnp.concatenate([hi[:, dk:], lo[:, dk:]], axis=1), preferred_element_type=F32)
            bf = pf[:, :dk] + pf[:, dk:]
            bb = pb[:, :dk] + pb[:, dk:]
            rf, tf = bf[mid - 1:mid, :], bf[c_len - 1:c_len, :]
            rb, tb = bb[mid:mid + 1, :], bb[0:1, :]
            k = k_at(rows).astype(F32)
            dec_f = jnp.exp(bf - rf)
            dec_b = jnp.exp(bb - rb)
            ks_f = k * (1.0 / dec_f)
            ks_b = k * (1.0 / dec_b)
            et_dst[i] = jnp.concatenate([jnp.exp(tf), jnp.exp(tb)], axis=1)
            if with_query:
                q = q_at(rows).astype(F32) * q_scale
                qs_f = q * dec_f
                qs_b = q * dec_b
                qs_scr[rows, :] = jnp.concatenate([qs_f, qs_b], axis=1).astype(BF16)
                qe_scr[rows, :] = jnp.concatenate([qs_f * jnp.exp(rf), qs_b * jnp.exp(rb)], axis=1).astype(BF16)
                kst_scr[:dk, rows] = ks_f.T.astype(BF16)
                kst_scr[dk:, rows] = ks_b.T.astype(BF16)
            return jnp.concatenate([ks_f * jnp.exp(tf - rf), ks_b * jnp.exp(tb - rb)], axis=1).astype(BF16)

        def state_increment(i, kd):
            pt_dst[i] = lax.dot_general(v_at(rows_of(i)), kd, TN_DIMS, preferred_element_type=F32)

        return [scaled_operands, state_increment]

    ctx_stages = prepare_stages(kc_at, vc_at, hic_at, loc_at, etc_scr, ptc_scr, False)
    lat_stages = prepare_stages(k_at, v_at, hi_at, lo_at, et_scr, pt_scr, True)
    both = [lambda i, arg, c=c, l=l: c(i, arg) if i < n_ctx_chunks else l(i - n_ctx_chunks, arg)
            for c, l in zip(ctx_stages, lat_stages)]
    _run_staggered(both, n_ctx_chunks + n_chunks, lag=GLA_PREPARE_LAG)

    s_f = jnp.zeros((dv, dk), F32)
    s_b = jnp.zeros((dv, dk), F32)
    for c in range(n_ctx_chunks):
        s_f = s_f * etc_scr[c, :, :dk] + ptc_scr[c, :, :dk]
    for c in reversed(range(n_ctx_chunks)):
        s_b = s_b * etc_scr[c, :, dk:] + ptc_scr[c, :, dk:]
    for c in range(n_chunks):
        sst_scr[c, :dk, :] = s_f.T.astype(BF16)
        s_f = s_f * et_scr[c, :, :dk] + pt_scr[c, :, :dk]
    for c in reversed(range(n_chunks)):
        sst_scr[c, dk:, :] = s_b.T.astype(BF16)
        s_b = s_b * et_scr[c, :, dk:] + pt_scr[c, :, dk:]

    def intra_scores(i, _):
        qs = qs_scr[rows_of(i), :]
        att_f = jnp.dot(qs[:, :dk], kst_scr[:dk, rows_of(i)], preferred_element_type=F32)
        att_b = jnp.dot(qs[:, dk:], kst_scr[dk:, rows_of(i)], preferred_element_type=F32)
        return (jnp.where(lower, att_f, 0.0) + jnp.where(upper, att_b, 0.0)).astype(BF16)

    def outputs(i, att):
        rows = rows_of(i)
        o = jnp.dot(att, v_at(rows), preferred_element_type=F32)
        o = o + jnp.dot(qe_scr[rows, :], sst_scr[i], preferred_element_type=F32)
        return o, jnp.mean(o * o, axis=-1, keepdims=True)

    def normalise(i, o_ms):
        rows = rows_of(i)
        on = o_ms[0] * lax.rsqrt(o_ms[1] + EPS)
        o_ref[0, rows, :] = ((on * hg_ref[...]) * gs_at(rows).astype(F32)).astype(o_ref.dtype)

    _run_staggered([intra_scores, outputs, normalise], n_chunks, lag=GLA_EMIT_LAG)


def _gla_scan(proj, proj_ctx, head_gain, riders, *, d_model):
    b, h, t, _ = proj.shape
    tc = proj_ctx.shape[2] // b
    kt, vt = d_model // 2, d_model
    dk, dv = kt // h, vt // h
    assert h == GLA_HEADS and t % GLA_CHUNK == 0 and tc % GLA_CHUNK == 0 and proj_ctx.shape[:3] == (1, h, b * tc)
    assert dk == V7X_LANES and dv == V7X_MXU_DIM and 2 * dk == V7X_MXU_DIM
    n_chunks, n_ctx_chunks = t // GLA_CHUNK, tc // GLA_CHUNK
    head_block = lambda a: pl.BlockSpec((1, 1) + a.shape[2:], lambda i, j: (i, j, 0, 0))
    ctx_block = pl.BlockSpec((1, 1, tc, proj_ctx.shape[3]), lambda i, j: (0, j, i, 0))
    flat = [r.reshape(-1, r.shape[-1]) for r in riders]
    slabs = [f.shape[0] // (b * h) for f in flat]
    bf16_rows = 16
    assert all(f.shape[0] == n * b * h and n % bf16_rows == 0 for f, n in zip(flat, slabs))
    rider_specs = [pl.BlockSpec((n, f.shape[1]), lambda i, j: (i * h + j, 0)) for f, n in zip(flat, slabs)]
    in_specs = [head_block(proj), ctx_block, pl.BlockSpec((1, dv), lambda i, j: (0, 0))] + rider_specs
    tok = lambda width: pltpu.VMEM((t, width), BF16)
    increments = lambda n: [pltpu.VMEM((n, dv, 2 * dk), F32), pltpu.VMEM((n, 1, 2 * dk), F32)]
    out, *cast = pl.pallas_call(
        functools.partial(_gla_kernel, n_riders=len(flat), n_chunks=n_chunks, n_ctx_chunks=n_ctx_chunks, dk=dk,
                          dv=dv, q_scale=float(dk) ** -0.5),
        grid=(b, h),
        in_specs=in_specs,
        out_specs=[pl.BlockSpec((1, t, dv), lambda i, j: (i, 0, j))] + rider_specs,
        out_shape=[jax.ShapeDtypeStruct((b, t, vt), BF16)] + [jax.ShapeDtypeStruct(f.shape, BF16) for f in flat],
        scratch_shapes=[tok(2 * dk), tok(2 * dk), pltpu.VMEM((2 * dk, t), BF16)] + increments(n_chunks)
                       + increments(n_ctx_chunks) + [pltpu.VMEM((n_chunks, 2 * dk, dv), BF16)],
        compiler_params=pltpu.CompilerParams(
            dimension_semantics=("parallel", "parallel"),
            vmem_limit_bytes=_vmem_limit(48 * 1024 * 1024)),
        name="gla_scan",
    )(proj, proj_ctx, head_gain.reshape(1, dv), *flat)
    return out, [c.reshape(r.shape) for c, r in zip(cast, riders)]


def _rolling_steps(n_tiles, head, body):
    s = pl.program_id(0)
    pl.when(s == 0)(head)
    pl.when(jnp.logical_and(s > 0, s < n_tiles))(functools.partial(body, True))
    pl.when(s == n_tiles)(functools.partial(body, False))


def _rolling_slots():
    head_slot = pl.program_id(0) % 2
    return head_slot, 1 - head_slot


def _head_tile(s, n_tiles):
    return jnp.minimum(s, n_tiles - 1)


def _body_tile(s):
    return jnp.maximum(s - 1, 0)


def _shift_rows(u, k):
    z = jnp.zeros((abs(k), u.shape[1]), u.dtype)
    if k > 0:
        return jnp.concatenate([z, u[:-k]], axis=0)
    return jnp.concatenate([u[-k:], z], axis=0)


def _ffn_kernel(x_ref, pre_ref, mix_gate_ref, shift_ref, scale_ref, gate_ref, gain_ref, wmix_ref,
                wup_ref, cw_ref, cb_ref, wdn_ref, fin_ref, o_ref, x_scr, hn_scr, *, hidden, final_norm,
                n_tiles):
    rows_img, cols, d = x_ref.shape[1:]
    tm = rows_img * cols
    head_slot, body_slot = _rolling_slots()

    def head_matmul():
        return jnp.dot(pre_ref[0].reshape(tm, pre_ref.shape[3]), wmix_ref[0], preferred_element_type=F32)

    def head_finish(y):
        x = x_ref[0].reshape(tm, d) + mix_gate_ref[0] * y
        x_scr[head_slot] = x
        hn_scr[head_slot] = _modulated_norm(x, gain_ref[...], shift_ref[0], scale_ref[0]).astype(BF16)

    def conv(u, c0):
        w = cw_ref[:, c0:c0 + HID_CHUNK]
        return (_shift_rows(u, cols) * w[0:1] + u * w[1:2] + _shift_rows(u, -cols) * w[2:3]
                + cb_ref[:, c0:c0 + HID_CHUNK])

    def body(with_head):
        acts = []
        head_y = []

        def up_proj(j, _):
            c0 = j * HID_CHUNK
            hn = hn_scr[body_slot]
            u = (jnp.dot(hn, wup_ref[0, :, c0:c0 + HID_CHUNK], preferred_element_type=F32),
                 jnp.dot(hn, wup_ref[0, :, hidden + c0:hidden + c0 + HID_CHUNK], preferred_element_type=F32))
            if with_head and j == 1:
                head_y.append(head_matmul())
            return u

        def conv_gate(j, u):
            c0 = j * HID_CHUNK
            acts.append((conv(u[0], c0) * _silu(conv(u[1], hidden + c0))).astype(BF16))
            if with_head and j == 1:
                head_finish(head_y.pop())

        _run_staggered([up_proj, conv_gate], hidden // HID_CHUNK, newest_first=True)
        down = jnp.dot(jnp.concatenate(acts, axis=1), wdn_ref[0], preferred_element_type=F32)
        out = x_scr[body_slot] + gate_ref[0] * down
        if final_norm:
            out = out * lax.rsqrt(jnp.mean(out * out, axis=-1, keepdims=True) + EPS) * fin_ref[...]
        o_ref[0] = out.reshape(rows_img, cols, d)

    _rolling_steps(n_tiles, lambda: head_finish(head_matmul()), body)


def _mixer_out_conv_ffn(x, pre, mod, w_mix, gain, w_up, conv_w, conv_b, w_down, fin_gain, *, mix_layer, layer,
                        final_norm):
    b, t, d = x.shape
    hidden = w_down.shape[1]
    assert t % GRID_W == 0 and GRID_W % FFN_COLS == 0 and hidden % HID_CHUNK == 0 and hidden >= 2 * HID_CHUNK
    rows_img = t // GRID_W
    col_tiles = GRID_W // FFN_COLS
    n_tiles = b * col_tiles
    tm = rows_img * FFN_COLS
    x4 = x.reshape(b, rows_img, GRID_W, d)
    pre4 = pre.reshape(b, rows_img, GRID_W, pre.shape[2])
    head = lambda s: _head_tile(s, n_tiles)
    back = _body_tile
    head_tile = lambda width: pl.BlockSpec((1, rows_img, FFN_COLS, width),
                                           lambda s: (head(s) // col_tiles, 0, head(s) % col_tiles, 0))
    head_mod = lambda idx: pl.BlockSpec((1, 1, d), lambda s: (head(s) // col_tiles, 0, idx))
    const = _resident
    out = pl.pallas_call(
        functools.partial(_ffn_kernel, hidden=hidden, final_norm=final_norm, n_tiles=n_tiles),
        grid=(n_tiles + 1,),
        in_specs=[head_tile(d), head_tile(pre.shape[2]), head_mod(2), head_mod(3), head_mod(4),
                  pl.BlockSpec((1, 1, d), lambda s: (back(s) // col_tiles, 0, 5)),
                  const((1, d)), _resident_layer(w_mix.shape, mix_layer), _resident_layer(w_up.shape, layer),
                  const(conv_w.shape), const((1, 2 * hidden)), _resident_layer(w_down.shape, layer),
                  const((1, d))],
        out_specs=pl.BlockSpec((1, rows_img, FFN_COLS, d),
                               lambda s: (back(s) // col_tiles, 0, back(s) % col_tiles, 0),
                               pipeline_mode=pl.Buffered(1)),
        out_shape=jax.ShapeDtypeStruct(x4.shape, F32),
        scratch_shapes=[pltpu.VMEM((2, tm, d), F32), pltpu.VMEM((2, tm, d), BF16)],
        compiler_params=pltpu.CompilerParams(
            dimension_semantics=("arbitrary",),
            vmem_limit_bytes=_vmem_limit(60 * 1024 * 1024)),
        name="conv_ffn",
    )(x4, pre4, mod, mod, mod, mod, gain.reshape(1, d), w_mix, w_up, conv_w, conv_b.reshape(1, 2 * hidden),
      w_down, fin_gain.reshape(1, d))
    return out.reshape(b, t, d)


def _short_conv_kernel(x_ref, shift_ref, scale_ref, gain_ref, win_ref, cw_ref, o_ref, *, width):
    x = x_ref[0]
    tm = x.shape[0]
    hn = _modulated_norm(x, gain_ref[...], shift_ref[0], scale_ref[0]).astype(BF16)
    pos = lax.broadcasted_iota(jnp.int32, (tm, SC_CHUNK), 0) % GRID_W
    has_left = pos != 0
    has_right = pos != GRID_W - 1
    for c0 in range(0, width, SC_CHUNK):
        cg, v, bg = (jnp.dot(hn, win_ref[0, :, base + c0:base + c0 + SC_CHUNK], preferred_element_type=F32)
                     for base in (width, 2 * width, 0))
        u = cg * v
        w = cw_ref[:, c0:c0 + SC_CHUNK]
        left = jnp.where(has_left, pltpu.roll(u, 1, axis=0), 0.0)
        right = jnp.where(has_right, pltpu.roll(u, tm - 1, axis=0), 0.0)
        o_ref[0, :, c0:c0 + SC_CHUNK] = (bg * (left * w[0:1] + u * w[1:2] + right * w[2:3])).astype(o_ref.dtype)


def _short_conv(x, mod, gain, w_in, conv_w, *, layer, tm):
    b, t, d = x.shape
    width = conv_w.shape[1]
    assert t % tm == 0 and tm % GRID_W == 0 and width % SC_CHUNK == 0
    mod_spec = lambda idx: pl.BlockSpec((1, 1, d), lambda i, j: (i, 0, idx))
    const = _resident
    tile = pl.BlockSpec((1, tm, d), lambda i, j: (i, j, 0))
    return pl.pallas_call(
        functools.partial(_short_conv_kernel, width=width),
        grid=(b, t // tm),
        in_specs=[tile, mod_spec(0), mod_spec(1), const((1, d)), _resident_layer(w_in.shape, layer),
                  const(conv_w.shape)],
        out_specs=pl.BlockSpec((1, tm, width), lambda i, j: (i, j, 0)),
        out_shape=jax.ShapeDtypeStruct((b, t, width), BF16),
        compiler_params=pltpu.CompilerParams(
            dimension_semantics=("parallel", "parallel"),
            vmem_limit_bytes=_vmem_limit(56 * 1024 * 1024)),
        name="short_conv",
    )(x, mod, mod, gain.reshape(1, d), w_in, conv_w)


def kernel(x, c, ctx, c_ctx, ada_w, ada_b, norm_mix, norm_ffn, gla_w_in, gla_w_a2, gla_b_a,
           gla_head_norm, gla_w_out, sc_w_in, sc_conv_w, sc_w_out, ffn_w_up, ffn_conv_w,
           ffn_conv_b, ffn_w_down, final_norm):
    b, t, d = x.shape
    depth = ada_w.shape[0]
    assert depth == 2 and gla_w_in.shape[0] == 1 and sc_w_in.shape[0] == 1
    h = GLA_HEADS
    kt, vt, rank = d // 2, d, GLA_GATE_RANK
    dk = kt // h

    pad = (-(b + 1)) % 8
    cvec = jnp.concatenate([c, c_ctx[None, :], jnp.zeros((pad, d), F32)], axis=0)
    mods = _ada_mod(cvec, ada_w, ada_b)
    mod_x = [mods[i, :b].reshape(b, 1, N_MOD * d) for i in range(depth)]
    mod_ctx = mods[0, b:b + 1].reshape(1, 1, N_MOD * d)

    n_in = gla_w_in.shape[2]
    n_pad = (-n_in) % V7X_MXU_DIM
    w_in = jnp.pad(gla_w_in[0], ((0, 0), (0, n_pad))).astype(BF16)
    a2 = gla_w_a2[0].reshape(2, rank, h, dk)
    w2 = jnp.zeros((h, V7X_MXU_DIM, 2 * dk), F32)
    w2 = w2.at[:, :rank, :dk].set(a2[0].transpose(1, 0, 2))
    w2 = w2.at[:, rank:2 * rank, dk:].set(a2[1].transpose(1, 0, 2)).astype(BF16)
    w2 = w2.transpose(1, 0, 2).reshape(V7X_MXU_DIM, 2 * kt)
    ba = gla_b_a[0].reshape(2, h, dk).transpose(1, 0, 2).reshape(1, 2 * kt)

    ctx_flat = ctx.reshape(1, b * ctx.shape[1], d)
    proj_ctx = _gla_in_proj(ctx_flat, mod_ctx, norm_mix[0], w_in, w2, ba, kt=kt, vt=vt,
                            tm=math.gcd(ctx_flat.shape[1], TOKEN_TILE), with_query=False)
    proj_x = _gla_in_proj(x, mod_x[0], norm_mix[0], w_in, w2, ba, kt=kt, vt=vt, tm=TOKEN_TILE, with_query=True)
    o, (w_up, w_down, w_mix0, w_sc_in, w_mix1) = _gla_scan(
        proj_x, proj_ctx, gla_head_norm[0], [ffn_w_up, ffn_w_down, gla_w_out, sc_w_in, sc_w_out], d_model=d)
    hcur = _mixer_out_conv_ffn(x, o, mod_x[0], w_mix0, norm_ffn[0], w_up, ffn_conv_w[0], ffn_conv_b[0], w_down,
                               final_norm, mix_layer=0, layer=0, final_norm=False)

    y = _short_conv(hcur, mod_x[1], norm_mix[1], w_sc_in, sc_conv_w[0], layer=0, tm=TOKEN_TILE)
    return _mixer_out_conv_ffn(hcur, y, mod_x[1], w_mix1, norm_ffn[1], w_up, ffn_conv_w[1], ffn_conv_b[1], w_down,
                               final_norm, mix_layer=0, layer=1, final_norm=True)
```
